```python
import math
import jax
import jax.numpy as jnp
from jax import lax
import numpy as np

D_MODEL = 1024
BATCH = 16
SEQ = 2048
DEPTH = 4

F32 = jnp.float32
N_MIXERS = 4
GRID_W = 64
MEM_LEN = 256
NORM_EPS = 1e-6

ATT_HEADS = 16
ATT_KV_HEADS = 4
ATT_HEAD_DIM = D_MODEL // ATT_HEADS
ATT_Q_W = ATT_HEADS * ATT_HEAD_DIM
ATT_KV_W = ATT_KV_HEADS * ATT_HEAD_DIM
ROPE_THETA = 10000.0
WINDOW = 128
ATT_BLOCK = 128

NA_ROWS_MAX = 8
NA_COLS = 16
NA_QCOLS = 16
NA_KCOLS = NA_QCOLS + NA_COLS

X_HEADS = 4
X_HEAD_DIM = 128
X_W = X_HEADS * X_HEAD_DIM

HG_KEY_DIM = 128
HG_HEADS = D_MODEL // HG_KEY_DIM
HG_VAL_DIM = D_MODEL // HG_HEADS
HG_W = HG_HEADS * HG_KEY_DIM
HG_CHUNK = 16

SSM_INNER = 2 * D_MODEL
SSM_HEAD_DIM = 64
SSM_HEADS = SSM_INNER // SSM_HEAD_DIM
SSM_GROUPS = 8
SSM_HEADS_PER_GROUP = SSM_HEADS // SSM_GROUPS
SSM_STATE = 128
SSM_CONV = 5
SSM_CHUNK = 64
SSM_CONV_DIM = SSM_INNER + 2 * SSM_GROUPS * SSM_STATE

D_FF = 2816
FFN_CONV = 3

A_IN_W = ATT_Q_W + 2 * ATT_KV_W + X_W
B_IN_W = 5 * HG_W + X_W
C_IN_W = SSM_INNER + SSM_CONV_DIM + 2 * SSM_HEADS + X_W
D_IN_W = ATT_Q_W + 2 * ATT_KV_W + X_W

kernel_name = 'hybrid_bidir_interleaved_encoder'


def _layers_of(m):
    return (DEPTH - m + N_MIXERS - 1) // N_MIXERS


def rmsnorm(x, g):
    xf = x.astype(F32)
    y = xf * lax.rsqrt(jnp.mean(xf * xf, axis=-1, keepdims=True) + NORM_EPS)
    return (y * g.astype(F32)).astype(x.dtype)


def flip_t(a):
    return jnp.flip(a, axis=1)


def rope(x, pos):
    half = x.shape[-1] // 2
    inv = ROPE_THETA ** (-jnp.arange(half, dtype=F32) / half)
    ang = pos.astype(F32)[:, None] * inv[None, :]
    cos = jnp.cos(ang)[None, :, None, :]
    sin = jnp.sin(ang)[None, :, None, :]
    xf = x.astype(F32)
    x1, x2 = xf[..., :half], xf[..., half:]
    return jnp.concatenate([x1 * cos - x2 * sin, x2 * cos + x1 * sin], axis=-1).astype(x.dtype)


def dwconv_centred(x, w, b):
    k, c = w.shape
    y = lax.conv_general_dilated(x, w[:, None, :].astype(x.dtype), window_strides=(1,),
                                 padding=[(k // 2, k // 2)],
                                 dimension_numbers=('NWC', 'WIO', 'NWC'),
                                 feature_group_count=c)
    return y + b.astype(x.dtype)


def memory_cross_attn(xq, mem_kv):
    bsz, t, _ = xq.shape
    q = xq.reshape(bsz, t, X_HEADS, X_HEAD_DIM)
    k, v = jnp.split(mem_kv, 2, axis=-1)
    k = k.reshape(bsz, -1, X_HEADS, X_HEAD_DIM)
    v = v.reshape(bsz, -1, X_HEADS, X_HEAD_DIM)
    s = jnp.einsum('bthd,bmhd->bhtm', q, k, preferred_element_type=F32) * (X_HEAD_DIM ** -0.5)
    p = jax.nn.softmax(s, axis=-1)
    o = jnp.einsum('bhtm,bmhd->bthd', p.astype(v.dtype), v)
    return o.reshape(bsz, t, X_W)


def window_gqa_sink(q, k, v, sink):
    bsz, t, hq, d = q.shape
    hkv = k.shape[2]
    grp = hq // hkv
    nb = t // ATT_BLOCK
    band = ATT_BLOCK + 2 * WINDOW
    pad = ((0, 0), (WINDOW, WINDOW), (0, 0), (0, 0))
    idx = np.arange(nb)[:, None] * ATT_BLOCK + np.arange(band)[None, :]
    kb = jnp.pad(k, pad)[:, idx]
    vb = jnp.pad(v, pad)[:, idx]
    qb = q.reshape(bsz, nb, ATT_BLOCK, hkv, grp, d)
    s = jnp.einsum('bnqhgd,bnkhd->bnhgqk', qb, kb, preferred_element_type=F32) * (d ** -0.5)
    qpos = np.arange(nb)[:, None] * ATT_BLOCK + np.arange(ATT_BLOCK)[None, :]
    kpos = idx - WINDOW
    valid = ((np.abs(qpos[:, :, None] - kpos[:, None, :]) <= WINDOW)
             & (kpos >= 0)[:, None, :] & (kpos < t)[:, None, :])
    s = jnp.where(valid[None, :, None, None], s, -jnp.inf)
    sk = sink.astype(F32).reshape(hkv, grp)[None, None, :, :, None, None]
    m = jnp.maximum(jnp.max(s, axis=-1, keepdims=True), sk)
    p = jnp.exp(s - m)
    p = p / (jnp.sum(p, axis=-1, keepdims=True) + jnp.exp(sk - m))
    o = jnp.einsum('bnhgqk,bnkhd->bnqhgd', p.astype(v.dtype), vb)
    return o.reshape(bsz, t, hq, d)


def neighbourhood_attn(q, k, v, rpb):
    bsz, t, hq, d = q.shape
    hkv = k.shape[2]
    grp = hq // hkv
    rows = t // GRID_W
    kr = min(NA_ROWS_MAX, rows)
    ncb = GRID_W // NA_QCOLS
    r = np.arange(rows)
    key_rows = np.clip(r - kr // 2, 0, rows - kr)[:, None] + np.arange(kr)[None, :]
    key_cols = (np.clip(np.arange(ncb) * NA_QCOLS - NA_COLS // 2, 0, GRID_W - NA_KCOLS)[:, None]
                + np.arange(NA_KCOLS)[None, :])
    key_idx = (key_rows[:, None, :, None] * GRID_W + key_cols[None, :, None, :]).reshape(rows, ncb, kr * NA_KCOLS)
    kg = k[:, key_idx]
    vg = v[:, key_idx]
    qb = q.reshape(bsz, rows, ncb, NA_QCOLS, hkv, grp, d)
    s = jnp.einsum('brjqhgd,brjkhd->brjhgqk', qb, kg, preferred_element_type=F32) * (d ** -0.5)
    qcol = np.arange(ncb)[:, None] * NA_QCOLS + np.arange(NA_QCOLS)[None, :]
    col_start = np.clip(qcol - NA_COLS // 2, 0, GRID_W - NA_COLS)
    kcol = np.repeat(key_cols[:, None, :], kr, axis=1).reshape(ncb, -1)
    krow = np.repeat(key_rows[:, :, None], NA_KCOLS, axis=2).reshape(rows, -1)
    in_win = ((kcol[:, None, :] >= col_start[..., None])
              & (kcol[:, None, :] < col_start[..., None] + NA_COLS))
    dr = krow - r[:, None] + NA_ROWS_MAX - 1
    dc = np.clip(kcol[:, None, :] - qcol[..., None] + NA_COLS - 1, 0, 2 * NA_COLS - 2)
    bias = rpb.astype(F32)[:, dr[:, None, None, :], dc[None]]
    bias = jnp.where(in_win[None, None], bias, -jnp.inf)
    bias = bias.reshape(hkv, grp, rows, ncb, NA_QCOLS, -1).transpose(2, 3, 0, 1, 4, 5)
    p = jax.nn.softmax(s + bias[None], axis=-1)
    o = jnp.einsum('brjhgqk,brjkhd->brjqhgd', p.astype(v.dtype), vg)
    return o.reshape(bsz, t, hq, d)


def gla_chunked(q, k, v, log_f):
    bsz, t, h, dk = q.shape
    dv = v.shape[-1]
    n = t // HG_CHUNK
    q, k, log_f = (a.reshape(bsz, n, HG_CHUNK, h, dk) for a in (q, k, log_f))
    v = v.reshape(bsz, n, HG_CHUNK, h, dv)
    b = jnp.cumsum(log_f, axis=2)
    b_end = b[:, :, -1:]
    q_dec = q * jnp.exp(b)
    k_dec = k * jnp.exp(-b)
    k_end = k * jnp.exp(b_end - b)
    causal = np.tril(np.ones((HG_CHUNK, HG_CHUNK), dtype=bool))
    s = jnp.where(causal, jnp.einsum('bcthk,bcshk->bchts', q_dec, k_dec), 0.0)
    o_intra = jnp.einsum('bchts,bcshv->bcthv', s, v)

    def step(state, inp):
        qd, ke, vc, dec = inp
        o = jnp.einsum('bthk,bhkv->bthv', qd, state)
        state = dec[..., None] * state + jnp.einsum('bshk,bshv->bhkv', ke, vc)
        return state, o

    xs = tuple(jnp.moveaxis(a, 1, 0) for a in (q_dec, k_end, v, jnp.exp(b_end[:, :, 0])))
    _, o_inter = lax.scan(step, jnp.zeros((bsz, h, dk, dv), F32), xs)
    return (o_intra + jnp.moveaxis(o_inter, 0, 1)).reshape(bsz, t, h, dv)


def ssd_chunked(x, dt, a, bm, cm):
    bsz, t, g, j, p = x.shape
    nst = bm.shape[-1]
    n = t // SSM_CHUNK
    x = x.reshape(bsz, n, SSM_CHUNK, g, j, p)
    dt = dt.reshape(bsz, n, SSM_CHUNK, g, j)
    bm = bm.reshape(bsz, n, SSM_CHUNK, g, nst)
    cm = cm.reshape(bsz, n, SSM_CHUNK, g, nst)
    cs = jnp.cumsum(dt * a, axis=2)
    xdt = x * dt[..., None]
    cs_t = jnp.moveaxis(cs, 2, -1)
    causal = np.tril(np.ones((SSM_CHUNK, SSM_CHUNK), dtype=bool))
    decay = jnp.exp(jnp.where(causal, cs_t[..., :, None] - cs_t[..., None, :], -jnp.inf))
    cb = jnp.einsum('bctgn,bcsgn->bcgts', cm, bm)
    y_diag = jnp.einsum('bcgjts,bcsgjp->bctgjp', cb[:, :, :, None] * decay, xdt)

    def step(state, inp):
        cq, bk, xd, c = inp
        c_end = c[:, -1]
        y_off = jnp.einsum('btgn,bgjpn->btgjp', cq, state) * jnp.exp(c)[..., None]
        w = jnp.exp(c_end[:, None] - c)[..., None]
        state = jnp.exp(c_end)[..., None, None] * state + jnp.einsum('bsgn,bsgjp->bgjpn', bk, xd * w)
        return state, y_off

    xs = tuple(jnp.moveaxis(arr, 1, 0) for arr in (cm, bm, xdt, cs))
    _, y_off = lax.scan(step, jnp.zeros((bsz, g, j, p, nst), F32), xs)
    return (y_diag + jnp.moveaxis(y_off, 0, 1)).reshape(bsz, t, g, j, p)


def hgrn_lower_bounds(logits):
    p = jax.nn.softmax(logits.astype(F32), axis=1)
    return jnp.cumsum(p, axis=1) - p[:, :1]


def mixer_window_attn(u, mem_kv, w_in, sink, w_out, pos):
    bsz, t, _ = u.shape
    q, k, v, xq = jnp.split(u @ w_in, [ATT_Q_W, ATT_Q_W + ATT_KV_W, ATT_Q_W + 2 * ATT_KV_W], axis=-1)
    q = rope(q.reshape(bsz, t, ATT_HEADS, ATT_HEAD_DIM), pos)
    k = rope(k.reshape(bsz, t, ATT_KV_HEADS, ATT_HEAD_DIM), pos)
    v = v.reshape(bsz, t, ATT_KV_HEADS, ATT_HEAD_DIM)
    o = window_gqa_sink(q, k, v, sink).reshape(bsz, t, ATT_Q_W)
    return jnp.concatenate([o, memory_cross_attn(xq, mem_kv)], axis=-1) @ w_out


def mixer_hgrn2(u, mem_kv, w_in, lb_fwd, lb_bwd, norm_g, w_out):
    bsz, t, _ = u.shape
    q, i, zf, zb, gate, xq = jnp.split(u @ w_in, [HG_W, 2 * HG_W, 3 * HG_W, 4 * HG_W, 5 * HG_W], axis=-1)
    shp = (bsz, t, HG_HEADS, HG_KEY_DIM)
    q = jax.nn.silu(q.astype(F32)).reshape(shp)
    v = i.astype(F32).reshape(bsz, t, HG_HEADS, HG_VAL_DIM)

    def forget(z, lb):
        f = lb + (1.0 - lb) * jax.nn.sigmoid(z.astype(F32))
        return jnp.log(f).reshape(shp), (1.0 - f).reshape(shp)

    lf_f, k_f = forget(zf, lb_fwd)
    lf_b, k_b = forget(zb, lb_bwd)
    o = (gla_chunked(q, k_f, v, lf_f)
         + flip_t(gla_chunked(flip_t(q), flip_t(k_b), flip_t(v), flip_t(lf_b))))
    o = rmsnorm(o, norm_g.reshape(HG_HEADS, HG_VAL_DIM)).reshape(bsz, t, HG_W).astype(u.dtype)
    o = o * jax.nn.silu(gate)
    return jnp.concatenate([o, memory_cross_attn(xq, mem_kv)], axis=-1) @ w_out


def mixer_mamba2(u, mem_kv, w_in, conv_w, conv_b, dt_bias, a_log, d_skip, norm_g, w_out):
    bsz, t, _ = u.shape
    z, xbc, dt_raw, xq = jnp.split(
        u @ w_in, [SSM_INNER, SSM_INNER + SSM_CONV_DIM, SSM_INNER + SSM_CONV_DIM + 2 * SSM_HEADS], axis=-1)
    xbc = jax.nn.silu(dwconv_centred(xbc, conv_w, conv_b)).astype(F32)
    xs, bm, cm = jnp.split(xbc, [SSM_INNER, SSM_INNER + SSM_GROUPS * SSM_STATE], axis=-1)
    xs = xs.reshape(bsz, t, SSM_GROUPS, SSM_HEADS_PER_GROUP, SSM_HEAD_DIM)
    bm = bm.reshape(bsz, t, SSM_GROUPS, SSM_STATE)
    cm = cm.reshape(bsz, t, SSM_GROUPS, SSM_STATE)
    dt = jax.nn.softplus(dt_raw.astype(F32).reshape(bsz, t, 2, SSM_HEADS) + dt_bias.astype(F32))
    dt = dt.reshape(bsz, t, 2, SSM_GROUPS, SSM_HEADS_PER_GROUP)
    a = -jnp.exp(a_log.astype(F32)).reshape(2, SSM_GROUPS, SSM_HEADS_PER_GROUP)
    y = (ssd_chunked(xs, dt[:, :, 0], a[0], bm, cm)
         + flip_t(ssd_chunked(flip_t(xs), flip_t(dt[:, :, 1]), a[1], flip_t(bm), flip_t(cm))))
    y = y + d_skip.astype(F32).reshape(SSM_GROUPS, SSM_HEADS_PER_GROUP, 1) * xs
    y = y.reshape(bsz, t, SSM_INNER) * jax.nn.silu(z.astype(F32))
    y = rmsnorm(y.reshape(bsz, t, SSM_GROUPS, -1), norm_g.reshape(SSM_GROUPS, -1))
    y = y.reshape(bsz, t, SSM_INNER).astype(u.dtype)
    return jnp.concatenate([y, memory_cross_attn(xq, mem_kv)], axis=-1) @ w_out


def mixer_neighbourhood(u, mem_kv, w_in, rpb, w_out):
    bsz, t, _ = u.shape
    q, k, v, xq = jnp.split(u @ w_in, [ATT_Q_W, ATT_Q_W + ATT_KV_W, ATT_Q_W + 2 * ATT_KV_W], axis=-1)
    q = q.reshape(bsz, t, ATT_HEADS, ATT_HEAD_DIM)
    k = k.reshape(bsz, t, ATT_KV_HEADS, ATT_HEAD_DIM)
    v = v.reshape(bsz, t, ATT_KV_HEADS, ATT_HEAD_DIM)
    o = neighbourhood_attn(q, k, v, rpb).reshape(bsz, t, ATT_Q_W)
    return jnp.concatenate([o, memory_cross_attn(xq, mem_kv)], axis=-1) @ w_out


def conv_ffn(v, w_in, conv_w, conv_b, w_out):
    gate, up = jnp.split(v @ w_in, 2, axis=-1)
    hid = jax.nn.gelu(dwconv_centred(gate, conv_w, conv_b)) * up
    return hid @ w_out


def setup_inputs(seed: int = 0) -> dict:
    key = jax.random.key(seed)
    ks = jax.random.split(key, 32)
    n_a, n_b, n_c, n_d = (_layers_of(m) for m in range(N_MIXERS))

    def nrm(k, shape, scale):
        return jax.random.normal(k, shape, F32) * scale

    def gain(k, shape):
        return 1.0 + 0.05 * jax.random.normal(k, shape, F32)

    dt0 = jnp.exp(jax.random.uniform(ks[16], (n_c, 2, SSM_HEADS), F32, math.log(1e-3), math.log(1e-1)))
    return {
        'x': nrm(ks[0], (BATCH, SEQ, D_MODEL), 1.0),
        'mem': nrm(ks[1], (BATCH, MEM_LEN, D_MODEL), 1.0),
        'norm_g': gain(ks[2], (DEPTH, 4, D_MODEL)),
        'mem_norm_g': gain(ks[3], (D_MODEL,)),
        'w_mem_kv': nrm(ks[4], (DEPTH, D_MODEL, 2 * X_W), D_MODEL ** -0.5),
        'a_w_in': nrm(ks[5], (n_a, D_MODEL, A_IN_W), D_MODEL ** -0.5),
        'a_sink': nrm(ks[6], (n_a, ATT_HEADS), 0.5),
        'a_w_out': nrm(ks[7], (n_a, ATT_Q_W + X_W, D_MODEL), (ATT_Q_W + X_W) ** -0.5),
        'b_w_in': nrm(ks[8], (n_b, D_MODEL, B_IN_W), D_MODEL ** -0.5),
        'b_lb_logits': nrm(ks[9], (2, DEPTH, HG_W), 0.1),
        'b_norm_g': gain(ks[10], (n_b, HG_W)),
        'b_w_out': nrm(ks[11], (n_b, HG_W + X_W, D_MODEL), (HG_W + X_W) ** -0.5),
        'c_w_in': nrm(ks[12], (n_c, D_MODEL, C_IN_W), D_MODEL ** -0.5),
        'c_conv_w': nrm(ks[13], (n_c, SSM_CONV, SSM_CONV_DIM), SSM_CONV ** -0.5),
        'c_conv_b': nrm(ks[14], (n_c, SSM_CONV_DIM), 0.02),
        'c_dt_bias': dt0 + jnp.log(-jnp.expm1(-dt0)),
        'c_a_log': jnp.log(jax.random.uniform(ks[15], (n_c, 2, SSM_HEADS), F32, 1.0, 16.0)),
        'c_d': 1.0 + 0.1 * jax.random.normal(ks[17], (n_c, SSM_HEADS), F32),
        'c_norm_g': gain(ks[18], (n_c, SSM_INNER)),
        'c_w_out': nrm(ks[19], (n_c, SSM_INNER + X_W, D_MODEL), (SSM_INNER + X_W) ** -0.5),
        'd_w_in': nrm(ks[20], (n_d, D_MODEL, D_IN_W), D_MODEL ** -0.5),
        'd_rpb': nrm(ks[21], (n_d, ATT_HEADS, 2 * NA_ROWS_MAX - 1, 2 * NA_COLS - 1), 0.1),
        'd_w_out': nrm(ks[22], (n_d, ATT_Q_W + X_W, D_MODEL), (ATT_Q_W + X_W) ** -0.5),
        'ffn_w_in': nrm(ks[23], (DEPTH, D_MODEL, 2 * D_FF), D_MODEL ** -0.5),
        'ffn_conv_w': nrm(ks[24], (DEPTH, FFN_CONV, D_FF), FFN_CONV ** -0.5),
        'ffn_conv_b': nrm(ks[25], (DEPTH, D_FF), 0.02),
        'ffn_w_out': nrm(ks[26], (DEPTH, D_FF, D_MODEL), D_FF ** -0.5),
    }


def reference(x, mem, norm_g, mem_norm_g, w_mem_kv, a_w_in, a_sink, a_w_out,
              b_w_in, b_lb_logits, b_norm_g, b_w_out,
              c_w_in, c_conv_w, c_conv_b, c_dt_bias, c_a_log, c_d, c_norm_g, c_w_out,
              d_w_in, d_rpb, d_w_out, ffn_w_in, ffn_conv_w, ffn_conv_b, ffn_w_out):
    t = x.shape[1]
    pos = jnp.arange(t)
    mem_n = rmsnorm(mem, mem_norm_g)
    lb = hgrn_lower_bounds(b_lb_logits)
    h = x
    for layer in range(DEPTH):
        kind, slot = layer % N_MIXERS, layer // N_MIXERS
        u = rmsnorm(h, norm_g[layer, 0])
        mem_kv = mem_n @ w_mem_kv[layer]
        if kind == 0:
            y = mixer_window_attn(u, mem_kv, a_w_in[slot], a_sink[slot], a_w_out[slot], pos)
        elif kind == 1:
            y = mixer_hgrn2(u, mem_kv, b_w_in[slot], lb[0, layer], lb[1, layer], b_norm_g[slot], b_w_out[slot])
        elif kind == 2:
            y = mixer_mamba2(u, mem_kv, c_w_in[slot], c_conv_w[slot], c_conv_b[slot], c_dt_bias[slot],
                             c_a_log[slot], c_d[slot], c_norm_g[slot], c_w_out[slot])
        else:
            y = mixer_neighbourhood(u, mem_kv, d_w_in[slot], d_rpb[slot], d_w_out[slot])
        h = h + rmsnorm(y, norm_g[layer, 1])
        f = conv_ffn(rmsnorm(h, norm_g[layer, 2]), ffn_w_in[layer], ffn_conv_w[layer],
                     ffn_conv_b[layer], ffn_w_out[layer])
        h = h + rmsnorm(f, norm_g[layer, 3])
    return h
```

```python
import functools
import math

import jax
import jax.numpy as jnp
import numpy as np
from jax import lax
from jax.experimental import pallas as pl
from jax.experimental.pallas import tpu as pltpu

F32 = jnp.float32
BF16 = jnp.bfloat16

D_MODEL = 1024
NORM_EPS = 1e-6
GRID_W = 64
ATT_HEADS = 16
ATT_KV_HEADS = 4
ATT_HEAD_DIM = 64
ATT_Q_W = 1024
ATT_KV_W = 256
ROPE_THETA = 10000.0
WINDOW = 128
NA_ROWS = 8
NA_COLS = 16
X_HEADS = 4
X_HEAD_DIM = 128
X_W = 512
HG_HEADS = 8
HG_DIM = 128
HG_W = 1024
SSM_INNER = 2048
SSM_HEADS = 32
SSM_HEAD_DIM = 64
SSM_GROUPS = 8
SSM_STATE = 128
SSM_CONV = 5
SSM_CONV_DIM = 4096
D_FF = 2816
DEPTH = 4

LANES = 128
BF16_SUBLANES = 16
VMEM_LIMIT_BYTES = 56 * 1024 * 1024

ROW_TILE = 512
COL_CHUNK = 512
FFN_CHUNK = 256
HALO = BF16_SUBLANES
GLA_CHUNK = 64
GLA_SUB = 16
SSD_CHUNK = 128
XATT_TILE = 512
NEG_BIG = -1e30


def _cparams(*sem):
    return pltpu.CompilerParams(dimension_semantics=sem, vmem_limit_bytes=VMEM_LIMIT_BYTES)


def _resident(shape):
    nd = len(shape)
    return pl.BlockSpec(shape, lambda *_: (0,) * nd, pipeline_mode=pl.Buffered(1))


def _rms_scale(x):
    return lax.rsqrt(jnp.mean(x * x, axis=-1, keepdims=True) + NORM_EPS)


def _sigmoid(x):
    return 1.0 / (1.0 + jnp.exp(-x))


def _silu(x):
    return x * _sigmoid(x)


def _dot(a, b):
    return jnp.dot(a, b, preferred_element_type=F32)


def _dot_nt(a, b):
    return lax.dot_general(a, b, (((1,), (1,)), ((), ())), preferred_element_type=F32)


def _dot_tn(a, b):
    return lax.dot_general(a, b, (((0,), (0,)), ((), ())), preferred_element_type=F32)


def _rms_matmul_kernel(x_ref, g_ref, w_ref, o_ref, *, chunk):
    x = x_ref[...]
    xn = (x * _rms_scale(x) * g_ref[...]).astype(BF16)
    width = o_ref.shape[1]
    for c0 in range(0, width, chunk):
        c1 = min(c0 + chunk, width)
        o_ref[:, c0:c1] = _dot(xn, w_ref[:, c0:c1]).astype(o_ref.dtype)


def rms_matmul(x, g, w, out_dtype, tm):
    m, k = x.shape
    width = w.shape[1]
    return pl.pallas_call(
        functools.partial(_rms_matmul_kernel, chunk=COL_CHUNK),
        grid=(m // tm,),
        in_specs=[pl.BlockSpec((tm, k), lambda i: (i, 0)),
                  _resident((1, k)),
                  _resident((k, width))],
        out_specs=pl.BlockSpec((tm, width), lambda i: (i, 0)),
        out_shape=jax.ShapeDtypeStruct((m, width), out_dtype),
        compiler_params=_cparams("parallel"),
        name="rms_matmul",
    )(x, g.reshape(1, k), w)


def _xattn_kernel(q_ref, kv_ref, o_ref):
    scale = X_HEAD_DIM ** -0.5
    outs = []
    for h in range(X_HEADS):
        sl = slice(h * X_HEAD_DIM, (h + 1) * X_HEAD_DIM)
        q = q_ref[:, sl]
        k = kv_ref[:, sl]
        v = kv_ref[:, X_W + h * X_HEAD_DIM:X_W + (h + 1) * X_HEAD_DIM]
        s = _dot_nt(q, k) * scale
        p = jnp.exp(s - jnp.max(s, axis=-1, keepdims=True))
        l = jnp.sum(p, axis=-1, keepdims=True)
        outs.append(_dot(p.astype(BF16), v) / l)
    o_ref[...] = jnp.concatenate(outs, axis=1).astype(o_ref.dtype)


def cross_attn(proj, xq_block, mem_kv, bsz, t):
    mem_len = mem_kv.shape[0] // bsz
    tq = min(XATT_TILE, t)
    nt = t // tq
    return pl.pallas_call(
        _xattn_kernel,
        grid=(bsz, nt),
        in_specs=[pl.BlockSpec((tq, X_W), lambda b, i: (b * nt + i, xq_block)),
                  pl.BlockSpec((mem_len, 2 * X_W), lambda b, i: (b, 0))],
        out_specs=pl.BlockSpec((tq, X_W), lambda b, i: (b * nt + i, 0)),
        out_shape=jax.ShapeDtypeStruct((bsz * t, X_W), BF16),
        compiler_params=_cparams("parallel", "parallel"),
        name="cross_attn",
    )(proj, mem_kv)


def _proj_norm_residual(o_bf16, xo_ref, w1_ref, w2_ref, h_ref, g_ref, out_ref):
    y = _dot(o_bf16, w1_ref[...]) + _dot(xo_ref[...], w2_ref[...])
    out_ref[...] = h_ref[...] + y * _rms_scale(y) * g_ref[...]


def _out_plain_kernel(o_ref, xo_ref, w1_ref, w2_ref, h_ref, g_ref, out_ref):
    _proj_norm_residual(o_ref[...], xo_ref, w1_ref, w2_ref, h_ref, g_ref, out_ref)


def _out_hgrn_kernel(of_ref, ob_ref, gate_ref, ng_ref, xo_ref, w1_ref, w2_ref, h_ref, g_ref, out_ref):
    o = of_ref[...].astype(F32) + ob_ref[...].astype(F32)
    parts = []
    for hd in range(HG_HEADS):
        sl = slice(hd * HG_DIM, (hd + 1) * HG_DIM)
        oh = o[:, sl]
        on = oh * _rms_scale(oh) * ng_ref[:, sl]
        parts.append((on * _silu(gate_ref[:, sl].astype(F32))).astype(BF16))
    _proj_norm_residual(jnp.concatenate(parts, axis=1), xo_ref, w1_ref, w2_ref, h_ref, g_ref, out_ref)


def _out_ssd_kernel(yf_ref, yb_ref, z_ref, ng_ref, xo_ref, w1_ref, w2_ref, h_ref, g_ref, out_ref):
    gw = SSM_INNER // SSM_GROUPS
    parts = []
    for gi in range(SSM_GROUPS):
        sl = slice(gi * gw, (gi + 1) * gw)
        y = (yf_ref[:, sl].astype(F32) + yb_ref[:, sl].astype(F32)) * _silu(z_ref[:, sl].astype(F32))
        parts.append((y * _rms_scale(y) * ng_ref[:, sl]).astype(BF16))
    _proj_norm_residual(jnp.concatenate(parts, axis=1), xo_ref, w1_ref, w2_ref, h_ref, g_ref, out_ref)


def _row_spec(tm, width, col_block=0):
    return pl.BlockSpec((tm, width), lambda i: (i, col_block))


def out_proj(kind, acts, xo, w1, w2, h, g, tm):
    n = h.shape[0]
    kernels = {"plain": _out_plain_kernel, "hgrn": _out_hgrn_kernel, "ssd": _out_ssd_kernel}
    in_specs, args = [], []
    for arr, width, cb in acts:
        if width is None:
            in_specs.append(_resident(arr.shape))
        else:
            in_specs.append(_row_spec(tm, width, cb))
        args.append(arr)
    in_specs += [_row_spec(tm, X_W), _resident(w1.shape), _resident(w2.shape),
                 _row_spec(tm, D_MODEL), _resident((1, D_MODEL))]
    args += [xo, w1, w2, h, g.reshape(1, D_MODEL)]
    return pl.pallas_call(
        kernels[kind],
        grid=(n // tm,),
        in_specs=in_specs,
        out_specs=_row_spec(tm, D_MODEL),
        out_shape=jax.ShapeDtypeStruct((n, D_MODEL), F32),
        compiler_params=_cparams("parallel"),
        name="out_proj_" + kind,
    )(*args)


def _gelu_tanh(x):
    c = math.sqrt(2.0 / math.pi)
    return x * (0.5 * (1.0 + jnp.tanh(c * (x + 0.044715 * (x * x * x)))))


def _ffn_kernel(h_ref, hp_ref, hn_ref, g2_ref, g3_ref, wg_ref, wu_ref, cw_ref, cb_ref, wo_ref, out_ref,
                v_ref, gate_ref, acc_ref, *, tiles_per_seq, n_chunks):
    tm = h_ref.shape[0]
    tpos = pl.program_id(0) % tiles_per_seq

    def normed(x):
        return (x * _rms_scale(x) * g2_ref[...]).astype(BF16)

    v_ref[HALO:HALO + tm, :] = normed(h_ref[...])
    v_ref[0:HALO, :] = jnp.where(tpos == 0, jnp.zeros((HALO, D_MODEL), BF16), normed(hp_ref[...]))
    v_ref[HALO + tm:, :] = jnp.where(tpos == tiles_per_seq - 1, jnp.zeros((HALO, D_MODEL), BF16),
                                     normed(hn_ref[...]))
    acc_ref[...] = jnp.zeros_like(acc_ref)

    def body(c, carry):
        off = pl.multiple_of(c * FFN_CHUNK, FFN_CHUNK)
        gate_ref[...] = _dot(v_ref[...], wg_ref[:, pl.ds(off, FFN_CHUNK)])
        up = _dot(v_ref[HALO:HALO + tm, :], wu_ref[:, pl.ds(off, FFN_CHUNK)])
        cw = cw_ref[:, pl.ds(off, FFN_CHUNK)]
        conv = (cw[0:1] * gate_ref[HALO - 1:HALO - 1 + tm, :]
                + cw[1:2] * gate_ref[HALO:HALO + tm, :]
                + cw[2:3] * gate_ref[HALO + 1:HALO + 1 + tm, :]
                + cb_ref[:, pl.ds(off, FFN_CHUNK)])
        hid = (_gelu_tanh(conv) * up).astype(BF16)
        acc_ref[...] += _dot(hid, wo_ref[pl.ds(off, FFN_CHUNK), :])
        return carry

    lax.fori_loop(0, n_chunks, body, 0)
    f = acc_ref[...]
    out_ref[...] = h_ref[...] + f * _rms_scale(f) * g3_ref[...]


def conv_ffn(h, g2, g3, wg, wu, cw, cb, wo, t, tm):
    n = h.shape[0]
    tiles_per_seq = t // tm
    hb = tm // HALO
    last_hb = n // HALO - 1
    kern = functools.partial(_ffn_kernel, tiles_per_seq=tiles_per_seq, n_chunks=D_FF // FFN_CHUNK)
    return pl.pallas_call(
        kern,
        grid=(n // tm,),
        in_specs=[pl.BlockSpec((tm, D_MODEL), lambda i: (i, 0)),
                  pl.BlockSpec((HALO, D_MODEL), lambda i: (jnp.maximum(i * hb - 1, 0), 0)),
                  pl.BlockSpec((HALO, D_MODEL), lambda i: (jnp.minimum((i + 1) * hb, last_hb), 0)),
                  _resident((1, D_MODEL)), _resident((1, D_MODEL)),
                  _resident(wg.shape), _resident(wu.shape),
                  _resident(cw.shape), _resident((1, D_FF)), _resident(wo.shape)],
        out_specs=pl.BlockSpec((tm, D_MODEL), lambda i: (i, 0)),
        out_shape=jax.ShapeDtypeStruct((n, D_MODEL), F32),
        scratch_shapes=[pltpu.VMEM((tm + 2 * HALO, D_MODEL), BF16),
                        pltpu.VMEM((tm + 2 * HALO, FFN_CHUNK), F32),
                        pltpu.VMEM((tm, D_MODEL), F32)],
        compiler_params=_cparams("parallel"),
        name="conv_ffn",
    )(h, h, h, g2.reshape(1, D_MODEL), g3.reshape(1, D_MODEL), wg, wu, cw, cb.reshape(1, D_FF), wo)


def _half_mask(shape, half):
    lane = lax.broadcasted_iota(jnp.int32, shape, len(shape) - 1)
    return (lane >= ATT_HEAD_DIM) if half else (lane < ATT_HEAD_DIM)


def _gqa_group(qcols, g, kslab, vslab, probs_fn):
    rows = qcols[0].shape[0]
    kv_half = g % 2
    keep = _half_mask((rows, LANES), kv_half)
    qs = []
    for i in range(4):
        qc = qcols[2 * g + i // 2]
        if i % 2 != kv_half:
            qc = pltpu.roll(qc, ATT_HEAD_DIM, 1)
        qs.append(jnp.where(keep, qc, 0.0).astype(BF16))
    s = _dot_nt(jnp.concatenate(qs, axis=0), kslab)
    ps, denoms = [], []
    for i in range(4):
        p, den = probs_fn(i, s[i * rows:(i + 1) * rows])
        ps.append(p.astype(BF16))
        denoms.append(den)
    pv = _dot(jnp.concatenate(ps, axis=0), vslab)
    outs = []
    for i in range(4):
        o = pv[i * rows:(i + 1) * rows] / denoms[i]
        if i % 2 != kv_half:
            o = pltpu.roll(o, ATT_HEAD_DIM, 1)
        outs.append(o)
    low = _half_mask((rows, LANES), 0)
    return jnp.where(low, outs[0], outs[1]), jnp.where(low, outs[2], outs[3])


def _rope_cols(x, cos, sin_signed):
    lane = lax.broadcasted_iota(jnp.int32, x.shape, 1)
    first = (lane % ATT_HEAD_DIM) < (ATT_HEAD_DIM // 2)
    partner = jnp.where(first, pltpu.roll(x, LANES - ATT_HEAD_DIM // 2, 1), pltpu.roll(x, ATT_HEAD_DIM // 2, 1))
    return x * cos + partner * sin_signed


def _attn_window_kernel(sink_ref, q_ref, k_ref, v_ref, cos_ref, sin_ref, o_ref, kp_ref, vp_ref):
    t = k_ref.shape[0]
    blk = q_ref.shape[0]
    band = blk + 2 * WINDOW
    qb = pl.program_id(1)

    @pl.when(qb == 0)
    def _():
        zeros = jnp.zeros((WINDOW, ATT_KV_W), BF16)
        kp_ref[0:WINDOW, :] = zeros
        kp_ref[WINDOW + t:, :] = zeros
        vp_ref[0:WINDOW, :] = zeros
        vp_ref[WINDOW + t:, :] = zeros
        for j in range(ATT_KV_W // LANES):
            sl = slice(j * LANES, (j + 1) * LANES)
            kp_ref[WINDOW:WINDOW + t, sl] = _rope_cols(k_ref[:, sl].astype(F32), cos_ref[...], sin_ref[...]).astype(BF16)
        vp_ref[WINDOW:WINDOW + t, :] = v_ref[...]

    r0 = pl.multiple_of(qb * blk, blk)
    cos = cos_ref[pl.ds(r0, blk), :]
    sin = sin_ref[pl.ds(r0, blk), :]
    scale = ATT_HEAD_DIM ** -0.5
    qcols = [_rope_cols(q_ref[:, j * LANES:(j + 1) * LANES].astype(F32), cos, sin) * scale
             for j in range(ATT_Q_W // LANES)]

    ii = lax.broadcasted_iota(jnp.int32, (blk, band), 0)
    jj = lax.broadcasted_iota(jnp.int32, (blk, band), 1)
    kpos = r0 - WINDOW + jj
    valid = (jj - ii >= 0) & (jj - ii <= 2 * WINDOW) & (kpos >= 0) & (kpos < t)

    for g in range(ATT_KV_HEADS):
        slab = slice((g // 2) * LANES, (g // 2 + 1) * LANES)
        kslab = kp_ref[pl.ds(r0, band), slab]
        vslab = vp_ref[pl.ds(r0, band), slab]

        def probs(i, s, g=g):
            sk = sink_ref[4 * g + i]
            s = jnp.where(valid, s, NEG_BIG)
            m = jnp.maximum(jnp.max(s, axis=-1, keepdims=True), sk)
            p = jnp.exp(s - m)
            return p, jnp.sum(p, axis=-1, keepdims=True) + jnp.exp(sk - m)

        o0, o1 = _gqa_group(qcols, g, kslab, vslab, probs)
        o_ref[:, (2 * g) * LANES:(2 * g + 1) * LANES] = o0.astype(o_ref.dtype)
        o_ref[:, (2 * g + 1) * LANES:(2 * g + 2) * LANES] = o1.astype(o_ref.dtype)


def attn_window(proj, sink, cos_t, sin_t, bsz, t):
    blk = WINDOW
    nb = t // blk
    grid_spec = pltpu.PrefetchScalarGridSpec(
        num_scalar_prefetch=1,
        grid=(bsz, nb),
        in_specs=[pl.BlockSpec((blk, ATT_Q_W), lambda b, i, s: (b * nb + i, 0)),
                  pl.BlockSpec((t, ATT_KV_W), lambda b, i, s: (b, ATT_Q_W // ATT_KV_W)),
                  pl.BlockSpec((t, ATT_KV_W), lambda b, i, s: (b, ATT_Q_W // ATT_KV_W + 1)),
                  pl.BlockSpec((t, LANES), lambda b, i, s: (0, 0)),
                  pl.BlockSpec((t, LANES), lambda b, i, s: (0, 0))],
        out_specs=pl.BlockSpec((blk, ATT_Q_W), lambda b, i, s: (b * nb + i, 0)),
        scratch_shapes=[pltpu.VMEM((t + 2 * WINDOW, ATT_KV_W), BF16),
                        pltpu.VMEM((t + 2 * WINDOW, ATT_KV_W), BF16)],
    )
    return pl.pallas_call(
        _attn_window_kernel,
        grid_spec=grid_spec,
        out_shape=jax.ShapeDtypeStruct((bsz * t, ATT_Q_W), BF16),
        compiler_params=_cparams("parallel", "arbitrary"),
        name="attn_window",
    )(sink, proj, proj, proj, cos_t, sin_t)


def _rope_tables(t):
    half = ATT_HEAD_DIM // 2
    inv = ROPE_THETA ** (-jnp.arange(half, dtype=F32) / half)
    ang = jnp.arange(t).astype(F32)[:, None] * inv[None, :]
    cos = jnp.cos(ang)
    sin = jnp.sin(ang)
    reps = LANES // ATT_HEAD_DIM
    cos_t = jnp.tile(jnp.concatenate([cos, cos], axis=1), (1, reps))
    sin_t = jnp.tile(jnp.concatenate([-sin, sin], axis=1), (1, reps))
    return cos_t, sin_t


def _attn_nbr_kernel(q_ref, k_ref, v_ref, bias_ref, o_ref, *, rows, kr):
    r = pl.program_id(1)
    rs = jnp.clip(r - kr // 2, 0, rows - kr)
    k0 = pl.multiple_of(rs * GRID_W, GRID_W)
    nk = kr * GRID_W
    scale = ATT_HEAD_DIM ** -0.5
    qcols = [q_ref[:, j * LANES:(j + 1) * LANES].astype(F32) * scale for j in range(ATT_Q_W // LANES)]
    for g in range(ATT_KV_HEADS):
        slab = slice((g // 2) * LANES, (g // 2 + 1) * LANES)
        kslab = k_ref[pl.ds(k0, nk), slab]
        vslab = v_ref[pl.ds(k0, nk), slab]

        def probs(i, s, g=g):
            s = s + bias_ref[0, 4 * g + i]
            p = jnp.exp(s - jnp.max(s, axis=-1, keepdims=True))
            return p, jnp.sum(p, axis=-1, keepdims=True)

        o0, o1 = _gqa_group(qcols, g, kslab, vslab, probs)
        o_ref[:, (2 * g) * LANES:(2 * g + 1) * LANES] = o0.astype(o_ref.dtype)
        o_ref[:, (2 * g + 1) * LANES:(2 * g + 2) * LANES] = o1.astype(o_ref.dtype)


def _nbr_bias_tables(rpb, kr):
    qc = np.arange(GRID_W)
    kc = np.arange(GRID_W)
    col_start = np.clip(qc - NA_COLS // 2, 0, GRID_W - NA_COLS)
    in_win = (kc[None, :] >= col_start[:, None]) & (kc[None, :] < col_start[:, None] + NA_COLS)
    dc = np.clip(kc[None, :] - qc[:, None] + NA_COLS - 1, 0, 2 * NA_COLS - 2)
    off = np.arange(kr)
    dr = np.arange(kr)[None, :] - off[:, None] + NA_ROWS - 1
    tab = rpb.astype(F32)[:, dr[:, :, None, None], dc[None, None]]
    tab = jnp.where(in_win[None, None, None], tab, NEG_BIG)
    tab = tab.transpose(1, 0, 3, 2, 4)
    return tab.reshape(kr, ATT_HEADS, GRID_W, kr * GRID_W)


def attn_nbr(proj, rpb, bsz, t):
    rows = t // GRID_W
    kr = min(NA_ROWS, rows)
    bias = _nbr_bias_tables(rpb, kr)

    def bias_index(b, r):
        return (r - jnp.clip(r - kr // 2, 0, rows - kr), 0, 0, 0)

    return pl.pallas_call(
        functools.partial(_attn_nbr_kernel, rows=rows, kr=kr),
        grid=(bsz, rows),
        in_specs=[pl.BlockSpec((GRID_W, ATT_Q_W), lambda b, r: (b * rows + r, 0)),
                  pl.BlockSpec((t, ATT_KV_W), lambda b, r: (b, ATT_Q_W // ATT_KV_W)),
                  pl.BlockSpec((t, ATT_KV_W), lambda b, r: (b, ATT_Q_W // ATT_KV_W + 1)),
                  pl.BlockSpec((1, ATT_HEADS, GRID_W, kr * GRID_W), bias_index)],
        out_specs=pl.BlockSpec((GRID_W, ATT_Q_W), lambda b, r: (b * rows + r, 0)),
        out_shape=jax.ShapeDtypeStruct((bsz * t, ATT_Q_W), BF16),
        compiler_params=_cparams("parallel", "arbitrary"),
        name="attn_nbr",
    )(proj, proj, proj, bias)


def _scan_rows(x, reverse):
    n = x.shape[0]
    row = lax.broadcasted_iota(jnp.int32, x.shape, 0)
    s = 1
    while s < n:
        if reverse:
            x = x + jnp.where(row < n - s, pltpu.roll(x, n - s, 0), 0.0)
        else:
            x = x + jnp.where(row >= s, pltpu.roll(x, s, 0), 0.0)
        s *= 2
    return x


def _gla_direction(q_ref, v_ref, z_ref, lb, o_ref, st_ref, d, reverse):
    c = q_ref.shape[0]
    nsub = c // GLA_SUB
    q = _silu(q_ref[...].astype(F32))
    v = v_ref[...]
    f = lb + (1.0 - lb) * _sigmoid(z_ref[...])
    k = 1.0 - f
    b = _scan_rows(jnp.log(f), reverse)
    row = lax.broadcasted_iota(jnp.int32, b.shape, 0)
    blk = row // GLA_SUB

    e_own = jnp.zeros_like(b)
    lhs, refs = [], []
    for j in range(nsub):
        edge = j * GLA_SUB if reverse else (j + 1) * GLA_SUB - 1
        e_j = b[edge:edge + 1, :]
        refs.append(e_j)
        e_own = jnp.where(blk == j, e_j, e_own)
        live = (row < (j + 1) * GLA_SUB) if reverse else (row >= j * GLA_SUB)
        lhs.append(jnp.where(live, q * jnp.exp(b - e_j), 0.0).astype(BF16))
    k_own = k * jnp.exp(e_own - b)
    rhs = [jnp.where(blk == j, k_own, 0.0).astype(BF16) for j in range(nsub)]

    end = 0 if reverse else c - 1
    b_end = b[end:end + 1, :]
    q_in = (q * jnp.exp(b)).astype(BF16)
    k_end = (k * jnp.exp(b_end - b)).astype(BF16)
    dec = jnp.exp(b_end)

    ti = lax.broadcasted_iota(jnp.int32, (c, c), 0)
    si = lax.broadcasted_iota(jnp.int32, (c, c), 1)
    tri = (si >= ti) if reverse else (si <= ti)

    outs = []
    for hd in range(HG_HEADS):
        sl = slice(hd * HG_DIM, (hd + 1) * HG_DIM)
        a = _dot_nt(jnp.concatenate([p[:, sl] for p in lhs], axis=1),
                    jnp.concatenate([p[:, sl] for p in rhs], axis=1))
        a = jnp.where(tri, a, 0.0).astype(BF16)
        st = st_ref[d, hd]
        o = _dot(a, v[:, sl]) + _dot_nt(q_in[:, sl], st.astype(BF16))
        outs.append(o)
        st_ref[d, hd] = st * dec[:, sl] + _dot_tn(v[:, sl], k_end[:, sl])
    o_ref[...] = jnp.concatenate(outs, axis=1).astype(o_ref.dtype)


def _gla_kernel(qf_ref, vf_ref, zf_ref, qb_ref, vb_ref, zb_ref, lb_ref, of_ref, ob_ref, st_ref):
    @pl.when(pl.program_id(1) == 0)
    def _():
        st_ref[...] = jnp.zeros_like(st_ref)

    _gla_direction(qf_ref, vf_ref, zf_ref, lb_ref[0:1, :], of_ref, st_ref, 0, False)
    _gla_direction(qb_ref, vb_ref, zb_ref, lb_ref[1:2, :], ob_ref, st_ref, 1, True)


def gla_bidir(proj, z, lb, bsz, t):
    c = min(GLA_CHUNK, t)
    nch = t // c
    fwd = lambda col: pl.BlockSpec((c, HG_W), lambda b, j: (b * nch + j, col))
    bwd = lambda col: pl.BlockSpec((c, HG_W), lambda b, j: (b * nch + nch - 1 - j, col))
    out = jax.ShapeDtypeStruct((bsz * t, HG_W), BF16)
    return pl.pallas_call(
        _gla_kernel,
        grid=(bsz, nch),
        in_specs=[fwd(0), fwd(1), fwd(0), bwd(0), bwd(1), bwd(1), pl.BlockSpec((2, HG_W), lambda b, j: (0, 0))],
        out_specs=[fwd(0), bwd(0)],
        out_shape=[out, out],
        scratch_shapes=[pltpu.VMEM((2, HG_HEADS, HG_DIM, HG_DIM), F32)],
        compiler_params=_cparams("parallel", "arbitrary"),
        name="gla_bidir",
    )(proj, proj, z, proj, proj, z, lb)


def _expand_pair(cols, l0, shape):
    low = _half_mask(shape, 0)
    rows = shape[0]
    return jnp.where(low, jnp.broadcast_to(cols[:, l0:l0 + 1], (rows, LANES)),
                     jnp.broadcast_to(cols[:, l0 + 1:l0 + 2], (rows, LANES)))


def _ssd_direction(x_ref, xp_ref, xn_ref, dt_ref, cw_ref, cb_ref, dtb_ref, a_ref, dsk_ref, y_ref,
                   st_ref, ext_ref, act_ref, d, reverse, is_first, is_last):
    c = x_ref.shape[0]
    width = x_ref.shape[1]
    ext_ref[HALO:HALO + c, :] = x_ref[...].astype(F32)
    ext_ref[0:HALO, :] = jnp.where(is_first, 0.0, xp_ref[...].astype(F32))
    ext_ref[HALO + c:, :] = jnp.where(is_last, 0.0, xn_ref[...].astype(F32))
    slab_w = 4 * LANES
    pad = SSM_CONV // 2
    for c0 in range(0, width, slab_w):
        sl = slice(c0, c0 + slab_w)
        acc = cb_ref[:, sl] + cw_ref[0:1, sl] * ext_ref[HALO - pad:HALO - pad + c, sl]
        for kk in range(1, SSM_CONV):
            acc = acc + cw_ref[kk:kk + 1, sl] * ext_ref[HALO - pad + kk:HALO - pad + kk + c, sl]
        act_ref[:, sl] = _silu(acc)

    hoff = d * SSM_HEADS
    xdt = dt_ref[...] + dtb_ref[...]
    dt = jnp.maximum(xdt, 0.0) + jnp.log(1.0 + jnp.exp(-jnp.abs(xdt)))
    cs = _scan_rows(dt * a_ref[...], reverse)
    cs_t = cs.T
    end = 0 if reverse else c - 1
    cs_end = cs[end:end + 1, :]
    ecs = jnp.exp(cs)
    wdec = jnp.exp(cs_end - cs) * dt
    dec = jnp.exp(cs_end)

    ti = lax.broadcasted_iota(jnp.int32, (c, c), 0)
    si = lax.broadcasted_iota(jnp.int32, (c, c), 1)
    tri = (si >= ti) if reverse else (si <= ti)

    def decay(l):
        diff = jnp.broadcast_to(cs[:, l:l + 1], (c, c)) - jnp.broadcast_to(cs_t[l:l + 1, :], (c, c))
        return jnp.exp(jnp.where(tri, diff, NEG_BIG))

    low = _half_mask((c, LANES), 0)
    b_off = SSM_INNER
    c_off = SSM_INNER + SSM_GROUPS * SSM_STATE
    for g in range(SSM_GROUPS):
        bm = act_ref[:, b_off + g * SSM_STATE:b_off + (g + 1) * SSM_STATE].astype(BF16)
        cm = act_ref[:, c_off + g * SSM_STATE:c_off + (g + 1) * SSM_STATE].astype(BF16)
        cb = _dot_nt(cm, bm)
        for pp in range(2):
            p = 2 * g + pp
            l0 = hoff + 2 * p
            sl = slice(p * LANES, (p + 1) * LANES)
            xs = act_ref[:, sl]
            m = jnp.concatenate([cb * decay(l0), cb * decay(l0 + 1)], axis=1).astype(BF16)
            xd = xs * _expand_pair(dt, l0, (c, LANES))
            r = jnp.concatenate([jnp.where(low, xd, 0.0), jnp.where(low, 0.0, xd)], axis=0).astype(BF16)
            st = st_ref[d, g, :, pp * LANES:(pp + 1) * LANES]
            y = _dot(m, r) + _dot(cm, st.astype(BF16)) * _expand_pair(ecs, l0, (c, LANES))
            if not reverse:
                y = y + dsk_ref[:, sl] * xs
            y_ref[:, sl] = y.astype(y_ref.dtype)
            xw = (xs * _expand_pair(wdec, l0, (c, LANES))).astype(BF16)
            st_ref[d, g, :, pp * LANES:(pp + 1) * LANES] = (
                st * _expand_pair(dec, l0, (1, LANES)) + _dot_tn(bm, xw))


def _ssd_kernel(xf_ref, xfp_ref, xfn_ref, dtf_ref, xb_ref, xbp_ref, xbn_ref, dtb_in_ref,
                cw_ref, cb_ref, dtbias_ref, a_ref, dsk_ref, yf_ref, yb_ref, st_ref, ext_ref, act_ref, *, nch):
    j = pl.program_id(1)

    @pl.when(j == 0)
    def _():
        st_ref[...] = jnp.zeros_like(st_ref)

    _ssd_direction(xf_ref, xfp_ref, xfn_ref, dtf_ref, cw_ref, cb_ref, dtbias_ref, a_ref, dsk_ref, yf_ref,
                   st_ref, ext_ref, act_ref, 0, False, j == 0, j == nch - 1)
    _ssd_direction(xb_ref, xbp_ref, xbn_ref, dtb_in_ref, cw_ref, cb_ref, dtbias_ref, a_ref, dsk_ref, yb_ref,
                   st_ref, ext_ref, act_ref, 1, True, j == nch - 1, j == 0)


def ssd_bidir(proj, dtp, conv_w, conv_b, dt_bias, a_neg, d_skip, bsz, t):
    c = min(SSD_CHUNK, t)
    nch = t // c
    hb = c // HALO
    seq_hb = t // HALO
    w = SSM_CONV_DIM

    def specs(chunk_of):
        main = pl.BlockSpec((c, w), lambda b, j: (b * nch + chunk_of(j), 0))
        prev = pl.BlockSpec((HALO, w), lambda b, j: (b * seq_hb + jnp.maximum(chunk_of(j) * hb - 1, 0), 0))
        nxt = pl.BlockSpec((HALO, w), lambda b, j: (b * seq_hb + jnp.minimum((chunk_of(j) + 1) * hb, seq_hb - 1), 0))
        dts = pl.BlockSpec((c, LANES), lambda b, j: (b * nch + chunk_of(j), 0))
        return [main, prev, nxt, dts]

    fwd_of = lambda j: j
    bwd_of = lambda j: nch - 1 - j
    out_f = pl.BlockSpec((c, SSM_INNER), lambda b, j: (b * nch + j, 0))
    out_b = pl.BlockSpec((c, SSM_INNER), lambda b, j: (b * nch + nch - 1 - j, 0))
    out = jax.ShapeDtypeStruct((bsz * t, SSM_INNER), BF16)
    const = lambda shape: pl.BlockSpec(shape, lambda b, j: (0,) * len(shape))
    return pl.pallas_call(
        functools.partial(_ssd_kernel, nch=nch),
        grid=(bsz, nch),
        in_specs=specs(fwd_of) + specs(bwd_of) + [const((SSM_CONV, w)), const((1, w)), const((1, LANES)),
                                                  const((1, LANES)), const((1, SSM_INNER))],
        out_specs=[out_f, out_b],
        out_shape=[out, out],
        scratch_shapes=[pltpu.VMEM((2, SSM_GROUPS, SSM_STATE, 4 * SSM_HEAD_DIM), F32),
                        pltpu.VMEM((c + 2 * HALO, w), F32),
                        pltpu.VMEM((c, w), F32)],
        compiler_params=_cparams("parallel", "arbitrary"),
        name="ssd_bidir",
    )(proj, proj, proj, dtp, proj, proj, proj, dtp, conv_w, conv_b, dt_bias, a_neg, d_skip)


def _hgrn_lower_bounds(logits):
    p = jax.nn.softmax(logits.astype(F32), axis=1)
    return jnp.cumsum(p, axis=1) - p[:, :1]


def _pad_lanes(v, width=LANES):
    return jnp.pad(v, (0, width - v.shape[0])).reshape(1, width)


def kernel(x, mem, norm_g, mem_norm_g, w_mem_kv, a_w_in, a_sink, a_w_out, b_w_in, b_lb_logits, b_norm_g, b_w_out,
           c_w_in, c_conv_w, c_conv_b, c_dt_bias, c_a_log, c_d, c_norm_g, c_w_out, d_w_in, d_rpb, d_w_out,
           ffn_w_in, ffn_conv_w, ffn_conv_b, ffn_w_out):
    bsz, t, _ = x.shape
    n = bsz * t
    tm = min(ROW_TILE, t)
    depth = norm_g.shape[0]
    h = x.reshape(n, D_MODEL)
    mem2 = mem.reshape(-1, D_MODEL)
    cos_t, sin_t = _rope_tables(t)
    lb = _hgrn_lower_bounds(b_lb_logits)
    bf = lambda w: w.astype(BF16)

    for layer in range(depth):
        kind, slot = layer % 4, layer // 4
        mem_kv = rms_matmul(mem2, mem_norm_g, bf(w_mem_kv[layer]), BF16, min(ROW_TILE, mem2.shape[0]))
        g0, g1, g2, g3 = (norm_g[layer, i] for i in range(4))

        if kind == 0 or kind == 3:
            w_in = a_w_in[slot] if kind == 0 else d_w_in[slot]
            w_out = a_w_out[slot] if kind == 0 else d_w_out[slot]
            proj = rms_matmul(h, g0, bf(w_in), BF16, tm)
            if kind == 0:
                o = attn_window(proj, a_sink[slot].astype(F32), cos_t, sin_t, bsz, t)
            else:
                o = attn_nbr(proj, d_rpb[slot], bsz, t)
            xo = cross_attn(proj, (ATT_Q_W + 2 * ATT_KV_W) // X_W, mem_kv, bsz, t)
            h = out_proj("plain", [(o, ATT_Q_W, 0)], xo, bf(w_out[:ATT_Q_W]), bf(w_out[ATT_Q_W:]), h, g1, tm)
        elif kind == 1:
            w_in = b_w_in[slot]
            w_main = jnp.concatenate([w_in[:, :2 * HG_W], w_in[:, 4 * HG_W:]], axis=1)
            proj = rms_matmul(h, g0, bf(w_main), BF16, tm)
            z = rms_matmul(h, g0, bf(w_in[:, 2 * HG_W:4 * HG_W]), F32, tm)
            lbs = jnp.stack([lb[0, layer], lb[1, layer]])
            of, ob = gla_bidir(proj, z, lbs, bsz, t)
            xo = cross_attn(proj, 3 * HG_W // X_W, mem_kv, bsz, t)
            w_out = b_w_out[slot]
            h = out_proj("hgrn", [(of, HG_W, 0), (ob, HG_W, 0), (proj, HG_W, 2), (b_norm_g[slot].reshape(1, HG_W), None, 0)],
                         xo, bf(w_out[:HG_W]), bf(w_out[HG_W:]), h, g1, tm)
        else:
            w_in = c_w_in[slot]
            z_w = w_in[:, :SSM_INNER]
            xbc_w = w_in[:, SSM_INNER:SSM_INNER + SSM_CONV_DIM]
            dt_w = w_in[:, SSM_INNER + SSM_CONV_DIM:SSM_INNER + SSM_CONV_DIM + 2 * SSM_HEADS]
            xq_w = w_in[:, SSM_INNER + SSM_CONV_DIM + 2 * SSM_HEADS:]
            proj = rms_matmul(h, g0, bf(jnp.concatenate([xbc_w, z_w, xq_w], axis=1)), BF16, tm)
            dtp = rms_matmul(h, g0, bf(jnp.pad(dt_w, ((0, 0), (0, LANES - 2 * SSM_HEADS)))), F32, tm)
            a_neg = -jnp.exp(c_a_log[slot].astype(F32)).reshape(-1)
            d_skip = jnp.repeat(c_d[slot].astype(F32), SSM_HEAD_DIM).reshape(1, SSM_INNER)
            yf, yb = ssd_bidir(proj, dtp, c_conv_w[slot].astype(F32), c_conv_b[slot].astype(F32).reshape(1, -1),
                               _pad_lanes(c_dt_bias[slot].astype(F32).reshape(-1)), _pad_lanes(a_neg), d_skip, bsz, t)
            xo = cross_attn(proj, (SSM_CONV_DIM + SSM_INNER) // X_W, mem_kv, bsz, t)
            w_out = c_w_out[slot]
            h = out_proj("ssd", [(yf, SSM_INNER, 0), (yb, SSM_INNER, 0), (proj, SSM_INNER, SSM_CONV_DIM // SSM_INNER),
                                 (c_norm_g[slot].reshape(1, SSM_INNER), None, 0)],
                         xo, bf(w_out[:SSM_INNER]), bf(w_out[SSM_INNER:]), h, g1, tm)

        fw = ffn_w_in[layer]
        h = conv_ffn(h, g2, g3, bf(fw[:, :D_FF]), bf(fw[:, D_FF:]), ffn_conv_w[layer].astype(F32),
                     ffn_conv_b[layer].astype(F32), bf(ffn_w_out[layer]), t, tm)
    return h.reshape(bsz, t, D_MODEL)
```

```python
import functools
import math

import jax
import jax.numpy as jnp
import numpy as np
from jax import lax
from jax.experimental import pallas as pl
from jax.experimental.pallas import tpu as pltpu

F32 = jnp.float32
BF16 = jnp.bfloat16

D_MODEL = 1024
NORM_EPS = 1e-6
GRID_W = 64
ATT_HEADS = 16
ATT_KV_HEADS = 4
ATT_HEAD_DIM = 64
ATT_Q_W = 1024
ATT_KV_W = 256
ROPE_THETA = 10000.0
WINDOW = 128
NA_ROWS = 8
NA_COLS = 16
X_HEADS = 4
X_HEAD_DIM = 128
X_W = 512
HG_HEADS = 8
HG_DIM = 128
HG_W = 1024
SSM_INNER = 2048
SSM_HEADS = 32
SSM_HEAD_DIM = 64
SSM_GROUPS = 8
SSM_STATE = 128
SSM_CONV = 5
SSM_CONV_DIM = 4096
D_FF = 2816
DEPTH = 4

LANES = 128
BF16_SUBLANES = 16
VMEM_LIMIT_BYTES = 56 * 1024 * 1024

ROW_TILE = 512
COL_CHUNK = 512
FFN_CHUNK = 256
HALO = BF16_SUBLANES
GLA_CHUNK = 64
GLA_SUB = 16
SSD_CHUNK = LANES // 2
XATT_TILE = 512
NEG_BIG = -1e30


def _cparams(*sem):
    return pltpu.CompilerParams(dimension_semantics=sem, vmem_limit_bytes=VMEM_LIMIT_BYTES)


def _resident(shape):
    nd = len(shape)
    return pl.BlockSpec(shape, lambda *_: (0,) * nd, pipeline_mode=pl.Buffered(1))


def _rms_scale(x):
    return lax.rsqrt(jnp.mean(x * x, axis=-1, keepdims=True) + NORM_EPS)


def _sigmoid(x):
    return 1.0 / (1.0 + jnp.exp(-x))


def _silu(x):
    return x * _sigmoid(x)


def _dot(a, b):
    return jnp.dot(a, b, preferred_element_type=F32)


def _dot_nt(a, b):
    return lax.dot_general(a, b, (((1,), (1,)), ((), ())), preferred_element_type=F32)


def _dot_tn(a, b):
    return lax.dot_general(a, b, (((0,), (0,)), ((), ())), preferred_element_type=F32)


def _rms_matmul_kernel(x_ref, g_ref, w_ref, o_ref, *, chunk):
    x = x_ref[...]
    xn = (x * _rms_scale(x) * g_ref[...]).astype(BF16)
    width = o_ref.shape[1]
    for c0 in range(0, width, chunk):
        c1 = min(c0 + chunk, width)
        o_ref[:, c0:c1] = _dot(xn, w_ref[:, c0:c1]).astype(o_ref.dtype)


def rms_matmul(x, g, w, out_dtype, tm):
    m, k = x.shape
    width = w.shape[1]
    return pl.pallas_call(
        functools.partial(_rms_matmul_kernel, chunk=COL_CHUNK),
        grid=(m // tm,),
        in_specs=[pl.BlockSpec((tm, k), lambda i: (i, 0)),
                  _resident((1, k)),
                  _resident((k, width))],
        out_specs=pl.BlockSpec((tm, width), lambda i: (i, 0)),
        out_shape=jax.ShapeDtypeStruct((m, width), out_dtype),
        compiler_params=_cparams("parallel"),
        name="rms_matmul",
    )(x, g.reshape(1, k), w)


def _xattn_kernel(q_ref, kv_ref, o_ref):
    scale = X_HEAD_DIM ** -0.5
    outs = []
    for h in range(X_HEADS):
        sl = slice(h * X_HEAD_DIM, (h + 1) * X_HEAD_DIM)
        q = q_ref[:, sl]
        k = kv_ref[:, sl]
        v = kv_ref[:, X_W + h * X_HEAD_DIM:X_W + (h + 1) * X_HEAD_DIM]
        s = _dot_nt(q, k) * scale
        p = jnp.exp(s - jnp.max(s, axis=-1, keepdims=True))
        l = jnp.sum(p, axis=-1, keepdims=True)
        outs.append(_dot(p.astype(BF16), v) / l)
    o_ref[...] = jnp.concatenate(outs, axis=1).astype(o_ref.dtype)


def cross_attn(proj, xq_block, mem_kv, bsz, t):
    mem_len = mem_kv.shape[0] // bsz
    tq = min(XATT_TILE, t)
    nt = t // tq
    return pl.pallas_call(
        _xattn_kernel,
        grid=(bsz, nt),
        in_specs=[pl.BlockSpec((tq, X_W), lambda b, i: (b * nt + i, xq_block)),
                  pl.BlockSpec((mem_len, 2 * X_W), lambda b, i: (b, 0))],
        out_specs=pl.BlockSpec((tq, X_W), lambda b, i: (b * nt + i, 0)),
        out_shape=jax.ShapeDtypeStruct((bsz * t, X_W), BF16),
        compiler_params=_cparams("parallel", "parallel"),
        name="cross_attn",
    )(proj, mem_kv)


def _proj_norm_residual(o_bf16, xo_ref, w1_ref, w2_ref, h_ref, g_ref, out_ref):
    y = _dot(o_bf16, w1_ref[...]) + _dot(xo_ref[...], w2_ref[...])
    out_ref[...] = h_ref[...] + y * _rms_scale(y) * g_ref[...]


def _out_plain_kernel(o_ref, xo_ref, w1_ref, w2_ref, h_ref, g_ref, out_ref):
    _proj_norm_residual(o_ref[...], xo_ref, w1_ref, w2_ref, h_ref, g_ref, out_ref)


def _out_hgrn_kernel(of_ref, ob_ref, gate_ref, ng_ref, xo_ref, w1_ref, w2_ref, h_ref, g_ref, out_ref):
    o = of_ref[...].astype(F32) + ob_ref[...].astype(F32)
    parts = []
    for hd in range(HG_HEADS):
        sl = slice(hd * HG_DIM, (hd + 1) * HG_DIM)
        oh = o[:, sl]
        on = oh * _rms_scale(oh) * ng_ref[:, sl]
        parts.append((on * _silu(gate_ref[:, sl].astype(F32))).astype(BF16))
    _proj_norm_residual(jnp.concatenate(parts, axis=1), xo_ref, w1_ref, w2_ref, h_ref, g_ref, out_ref)


def _out_ssd_kernel(yf_ref, yb_ref, z_ref, ng_ref, xo_ref, w1_ref, w2_ref, h_ref, g_ref, out_ref):
    gw = SSM_INNER // SSM_GROUPS
    parts = []
    for gi in range(SSM_GROUPS):
        sl = slice(gi * gw, (gi + 1) * gw)
        y = (yf_ref[:, sl].astype(F32) + yb_ref[:, sl].astype(F32)) * _silu(z_ref[:, sl].astype(F32))
        parts.append((y * _rms_scale(y) * ng_ref[:, sl]).astype(BF16))
    _proj_norm_residual(jnp.concatenate(parts, axis=1), xo_ref, w1_ref, w2_ref, h_ref, g_ref, out_ref)


def _row_spec(tm, width, col_block=0):
    return pl.BlockSpec((tm, width), lambda i: (i, col_block))


def out_proj(kind, acts, xo, w1, w2, h, g, tm):
    n = h.shape[0]
    kernels = {"plain": _out_plain_kernel, "hgrn": _out_hgrn_kernel, "ssd": _out_ssd_kernel}
    in_specs, args = [], []
    for arr, width, cb in acts:
        if width is None:
            in_specs.append(_resident(arr.shape))
        else:
            in_specs.append(_row_spec(tm, width, cb))
        args.append(arr)
    in_specs += [_row_spec(tm, X_W), _resident(w1.shape), _resident(w2.shape),
                 _row_spec(tm, D_MODEL), _resident((1, D_MODEL))]
    args += [xo, w1, w2, h, g.reshape(1, D_MODEL)]
    return pl.pallas_call(
        kernels[kind],
        grid=(n // tm,),
        in_specs=in_specs,
        out_specs=_row_spec(tm, D_MODEL),
        out_shape=jax.ShapeDtypeStruct((n, D_MODEL), F32),
        compiler_params=_cparams("parallel"),
        name="out_proj_" + kind,
    )(*args)


def _gelu_tanh(x):
    c = math.sqrt(2.0 / math.pi)
    return x * (0.5 * (1.0 + jnp.tanh(c * (x + 0.044715 * (x * x * x)))))


def _ffn_kernel(h_ref, hp_ref, hn_ref, g2_ref, g3_ref, wg_ref, wu_ref, cw_ref, cb_ref, wo_ref, out_ref,
                v_ref, gate_ref, hid_ref, *, tiles_per_seq, n_chunks):
    tm = h_ref.shape[0]
    tpos = pl.program_id(0) % tiles_per_seq

    def normed(x):
        return (x * _rms_scale(x) * g2_ref[...]).astype(BF16)

    v_ref[HALO:HALO + tm, :] = normed(h_ref[...])
    v_ref[0:HALO, :] = jnp.where(tpos == 0, jnp.zeros((HALO, D_MODEL), BF16), normed(hp_ref[...]))
    v_ref[HALO + tm:, :] = jnp.where(tpos == tiles_per_seq - 1, jnp.zeros((HALO, D_MODEL), BF16),
                                     normed(hn_ref[...]))

    for c in range(n_chunks):
        sl = slice(c * FFN_CHUNK, (c + 1) * FFN_CHUNK)
        gate = gate_ref.at[c % 2]
        gate[...] = _dot(v_ref[...], wg_ref[:, sl])
        up = _dot(v_ref[HALO:HALO + tm, :], wu_ref[:, sl])
        conv = (cw_ref[0:1, sl] * gate[HALO - 1:HALO - 1 + tm, :]
                + cw_ref[1:2, sl] * gate[HALO:HALO + tm, :]
                + cw_ref[2:3, sl] * gate[HALO + 1:HALO + 1 + tm, :]
                + cb_ref[:, sl])
        hid_ref[:, sl] = (_gelu_tanh(conv) * up).astype(BF16)

    f = _dot(hid_ref[...], wo_ref[...])
    out_ref[...] = h_ref[...] + f * _rms_scale(f) * g3_ref[...]


def conv_ffn(h, g2, g3, wg, wu, cw, cb, wo, t, tm):
    n = h.shape[0]
    tiles_per_seq = t // tm
    hb = tm // HALO
    last_hb = n // HALO - 1
    kern = functools.partial(_ffn_kernel, tiles_per_seq=tiles_per_seq, n_chunks=D_FF // FFN_CHUNK)
    return pl.pallas_call(
        kern,
        grid=(n // tm,),
        in_specs=[pl.BlockSpec((tm, D_MODEL), lambda i: (i, 0)),
                  pl.BlockSpec((HALO, D_MODEL), lambda i: (jnp.maximum(i * hb - 1, 0), 0)),
                  pl.BlockSpec((HALO, D_MODEL), lambda i: (jnp.minimum((i + 1) * hb, last_hb), 0)),
                  _resident((1, D_MODEL)), _resident((1, D_MODEL)),
                  _resident(wg.shape), _resident(wu.shape),
                  _resident(cw.shape), _resident((1, D_FF)), _resident(wo.shape)],
        out_specs=pl.BlockSpec((tm, D_MODEL), lambda i: (i, 0)),
        out_shape=jax.ShapeDtypeStruct((n, D_MODEL), F32),
        scratch_shapes=[pltpu.VMEM((tm + 2 * HALO, D_MODEL), BF16),
                        pltpu.VMEM((2, tm + 2 * HALO, FFN_CHUNK), F32),
                        pltpu.VMEM((tm, D_FF), BF16)],
        compiler_params=_cparams("parallel"),
        name="conv_ffn",
    )(h, h, h, g2.reshape(1, D_MODEL), g3.reshape(1, D_MODEL), wg, wu, cw, cb.reshape(1, D_FF), wo)


def _half_mask(shape, half):
    lane = lax.broadcasted_iota(jnp.int32, shape, len(shape) - 1)
    return (lane >= ATT_HEAD_DIM) if half else (lane < ATT_HEAD_DIM)


def _gqa_scores(qcols, g, kslab):
    rows = qcols[0].shape[0]
    kv_half = g % 2
    keep = _half_mask((rows, LANES), kv_half)
    qs = []
    for i in range(4):
        qc = qcols[2 * g + i // 2]
        if i % 2 != kv_half:
            qc = pltpu.roll(qc, ATT_HEAD_DIM, 1)
        qs.append(jnp.where(keep, qc, 0.0).astype(BF16))
    return _dot_nt(jnp.concatenate(qs, axis=0), kslab)


def _gqa_outputs(s, g, vslab, probs_fn):
    rows = s.shape[0] // 4
    kv_half = g % 2
    ps, denoms = [], []
    for i in range(4):
        p, den = probs_fn(i, s[i * rows:(i + 1) * rows])
        ps.append(p.astype(BF16))
        denoms.append(den)
    pv = _dot(jnp.concatenate(ps, axis=0), vslab)
    outs = []
    for i in range(4):
        o = pv[i * rows:(i + 1) * rows] / denoms[i]
        if i % 2 != kv_half:
            o = pltpu.roll(o, ATT_HEAD_DIM, 1)
        outs.append(o)
    low = _half_mask((rows, LANES), 0)
    return jnp.where(low, outs[0], outs[1]), jnp.where(low, outs[2], outs[3])


def _rope_cols(x, cos, sin_signed):
    lane = lax.broadcasted_iota(jnp.int32, x.shape, 1)
    first = (lane % ATT_HEAD_DIM) < (ATT_HEAD_DIM // 2)
    partner = jnp.where(first, pltpu.roll(x, LANES - ATT_HEAD_DIM // 2, 1), pltpu.roll(x, ATT_HEAD_DIM // 2, 1))
    return x * cos + partner * sin_signed


def _attn_window_kernel(sink_ref, q_ref, k_ref, v_ref, cos_ref, sin_ref, o_ref, kp_ref, vp_ref):
    t = k_ref.shape[0]
    blk = q_ref.shape[0]
    band = blk + 2 * WINDOW
    qb = pl.program_id(1)

    @pl.when(qb == 0)
    def _():
        zeros = jnp.zeros((WINDOW, ATT_KV_W), BF16)
        kp_ref[0:WINDOW, :] = zeros
        kp_ref[WINDOW + t:, :] = zeros
        vp_ref[0:WINDOW, :] = zeros
        vp_ref[WINDOW + t:, :] = zeros
        for j in range(ATT_KV_W // LANES):
            sl = slice(j * LANES, (j + 1) * LANES)
            kp_ref[WINDOW:WINDOW + t, sl] = _rope_cols(k_ref[:, sl].astype(F32), cos_ref[...], sin_ref[...]).astype(BF16)
        vp_ref[WINDOW:WINDOW + t, :] = v_ref[...]

    r0 = pl.multiple_of(qb * blk, blk)
    cos = cos_ref[pl.ds(r0, blk), :]
    sin = sin_ref[pl.ds(r0, blk), :]
    scale = ATT_HEAD_DIM ** -0.5
    qcols = [_rope_cols(q_ref[:, j * LANES:(j + 1) * LANES].astype(F32), cos, sin) * scale
             for j in range(ATT_Q_W // LANES)]

    ii = lax.broadcasted_iota(jnp.int32, (blk, band), 0)
    jj = lax.broadcasted_iota(jnp.int32, (blk, band), 1)
    kpos = r0 - WINDOW + jj
    valid = (jj - ii >= 0) & (jj - ii <= 2 * WINDOW) & (kpos >= 0) & (kpos < t)

    slabs = [slice((g // 2) * LANES, (g // 2 + 1) * LANES) for g in range(ATT_KV_HEADS)]
    scores = [_gqa_scores(qcols, g, kp_ref[pl.ds(r0, band), slabs[g]]) for g in range(ATT_KV_HEADS)]
    for g in range(ATT_KV_HEADS):
        vslab = vp_ref[pl.ds(r0, band), slabs[g]]

        def probs(i, s, g=g):
            sk = sink_ref[4 * g + i]
            s = jnp.where(valid, s, NEG_BIG)
            m = jnp.maximum(jnp.max(s, axis=-1, keepdims=True), sk)
            p = jnp.exp(s - m)
            return p, jnp.sum(p, axis=-1, keepdims=True) + jnp.exp(sk - m)

        o0, o1 = _gqa_outputs(scores[g], g, vslab, probs)
        o_ref[:, (2 * g) * LANES:(2 * g + 1) * LANES] = o0.astype(o_ref.dtype)
        o_ref[:, (2 * g + 1) * LANES:(2 * g + 2) * LANES] = o1.astype(o_ref.dtype)


def attn_window(proj, sink, cos_t, sin_t, bsz, t):
    blk = WINDOW
    nb = t // blk
    grid_spec = pltpu.PrefetchScalarGridSpec(
        num_scalar_prefetch=1,
        grid=(bsz, nb),
        in_specs=[pl.BlockSpec((blk, ATT_Q_W), lambda b, i, s: (b * nb + i, 0)),
                  pl.BlockSpec((t, ATT_KV_W), lambda b, i, s: (b, ATT_Q_W // ATT_KV_W)),
                  pl.BlockSpec((t, ATT_KV_W), lambda b, i, s: (b, ATT_Q_W // ATT_KV_W + 1)),
                  pl.BlockSpec((t, LANES), lambda b, i, s: (0, 0)),
                  pl.BlockSpec((t, LANES), lambda b, i, s: (0, 0))],
        out_specs=pl.BlockSpec((blk, ATT_Q_W), lambda b, i, s: (b * nb + i, 0)),
        scratch_shapes=[pltpu.VMEM((t + 2 * WINDOW, ATT_KV_W), BF16),
                        pltpu.VMEM((t + 2 * WINDOW, ATT_KV_W), BF16)],
    )
    return pl.pallas_call(
        _attn_window_kernel,
        grid_spec=grid_spec,
        out_shape=jax.ShapeDtypeStruct((bsz * t, ATT_Q_W), BF16),
        compiler_params=_cparams("parallel", "arbitrary"),
        name="attn_window",
    )(sink, proj, proj, proj, cos_t, sin_t)


def _rope_tables(t):
    half = ATT_HEAD_DIM // 2
    inv = ROPE_THETA ** (-jnp.arange(half, dtype=F32) / half)
    ang = jnp.arange(t).astype(F32)[:, None] * inv[None, :]
    cos = jnp.cos(ang)
    sin = jnp.sin(ang)
    reps = LANES // ATT_HEAD_DIM
    cos_t = jnp.tile(jnp.concatenate([cos, cos], axis=1), (1, reps))
    sin_t = jnp.tile(jnp.concatenate([-sin, sin], axis=1), (1, reps))
    return cos_t, sin_t


def _attn_nbr_kernel(q_ref, k_ref, v_ref, bias_ref, o_ref, *, rows, kr):
    r = pl.program_id(1)
    rs = jnp.clip(r - kr // 2, 0, rows - kr)
    k0 = pl.multiple_of(rs * GRID_W, GRID_W)
    nk = kr * GRID_W
    scale = ATT_HEAD_DIM ** -0.5
    qcols = [q_ref[:, j * LANES:(j + 1) * LANES].astype(F32) * scale for j in range(ATT_Q_W // LANES)]
    slabs = [slice((g // 2) * LANES, (g // 2 + 1) * LANES) for g in range(ATT_KV_HEADS)]
    scores = [_gqa_scores(qcols, g, k_ref[pl.ds(k0, nk), slabs[g]]) for g in range(ATT_KV_HEADS)]
    for g in range(ATT_KV_HEADS):
        vslab = v_ref[pl.ds(k0, nk), slabs[g]]

        def probs(i, s, g=g):
            s = s + bias_ref[0, 4 * g + i]
            p = jnp.exp(s - jnp.max(s, axis=-1, keepdims=True))
            return p, jnp.sum(p, axis=-1, keepdims=True)

        o0, o1 = _gqa_outputs(scores[g], g, vslab, probs)
        o_ref[:, (2 * g) * LANES:(2 * g + 1) * LANES] = o0.astype(o_ref.dtype)
        o_ref[:, (2 * g + 1) * LANES:(2 * g + 2) * LANES] = o1.astype(o_ref.dtype)


def _nbr_bias_tables(rpb, kr):
    qc = np.arange(GRID_W)
    kc = np.arange(GRID_W)
    col_start = np.clip(qc - NA_COLS // 2, 0, GRID_W - NA_COLS)
    in_win = (kc[None, :] >= col_start[:, None]) & (kc[None, :] < col_start[:, None] + NA_COLS)
    dc = np.clip(kc[None, :] - qc[:, None] + NA_COLS - 1, 0, 2 * NA_COLS - 2)
    pick = ((dc[:, :, None] == np.arange(2 * NA_COLS - 1)) & in_win[:, :, None]).astype(np.float32)
    by_off = jnp.stack([rpb.astype(F32)[:, NA_ROWS - 1 - off:NA_ROWS - 1 - off + kr] for off in range(kr)])
    tab = jnp.einsum('ohid,qkd->ohqik', by_off, pick, precision=lax.Precision.HIGHEST)
    tab = tab + jnp.where(in_win, 0.0, NEG_BIG).astype(F32)[None, None, :, None, :]
    return tab.reshape(kr, ATT_HEADS, GRID_W, kr * GRID_W)


def attn_nbr(proj, rpb, bsz, t):
    rows = t // GRID_W
    kr = min(NA_ROWS, rows)
    bias = _nbr_bias_tables(rpb, kr)

    def bias_index(b, r):
        return (r - jnp.clip(r - kr // 2, 0, rows - kr), 0, 0, 0)

    return pl.pallas_call(
        functools.partial(_attn_nbr_kernel, rows=rows, kr=kr),
        grid=(bsz, rows),
        in_specs=[pl.BlockSpec((GRID_W, ATT_Q_W), lambda b, r: (b * rows + r, 0)),
                  pl.BlockSpec((t, ATT_KV_W), lambda b, r: (b, ATT_Q_W // ATT_KV_W)),
                  pl.BlockSpec((t, ATT_KV_W), lambda b, r: (b, ATT_Q_W // ATT_KV_W + 1)),
                  pl.BlockSpec((1, ATT_HEADS, GRID_W, kr * GRID_W), bias_index)],
        out_specs=pl.BlockSpec((GRID_W, ATT_Q_W), lambda b, r: (b * rows + r, 0)),
        out_shape=jax.ShapeDtypeStruct((bsz * t, ATT_Q_W), BF16),
        compiler_params=_cparams("parallel", "arbitrary"),
        name="attn_nbr",
    )(proj, proj, proj, bias)


def _scan_rows(x, reverse):
    n = x.shape[0]
    row = lax.broadcasted_iota(jnp.int32, x.shape, 0)
    s = 1
    while s < n:
        if reverse:
            x = x + jnp.where(row < n - s, pltpu.roll(x, n - s, 0), 0.0)
        else:
            x = x + jnp.where(row >= s, pltpu.roll(x, s, 0), 0.0)
        s *= 2
    return x


def _gla_direction(q_ref, v_ref, z_ref, lb, o_ref, st_ref, d, reverse):
    c = q_ref.shape[0]
    nsub = c // GLA_SUB
    q = _silu(q_ref[...].astype(F32))
    v = v_ref[...]
    f = lb + (1.0 - lb) * _sigmoid(z_ref[...])
    k = 1.0 - f
    b = _scan_rows(jnp.log(f), reverse)
    row = lax.broadcasted_iota(jnp.int32, b.shape, 0)
    blk = row // GLA_SUB

    e_own = jnp.zeros_like(b)
    lhs, refs = [], []
    for j in range(nsub):
        edge = j * GLA_SUB if reverse else (j + 1) * GLA_SUB - 1
        e_j = b[edge:edge + 1, :]
        refs.append(e_j)
        e_own = jnp.where(blk == j, e_j, e_own)
        live = (row < (j + 1) * GLA_SUB) if reverse else (row >= j * GLA_SUB)
        lhs.append(jnp.where(live, q * jnp.exp(b - e_j), 0.0).astype(BF16))
    k_own = k * jnp.exp(e_own - b)
    rhs = [jnp.where(blk == j, k_own, 0.0).astype(BF16) for j in range(nsub)]

    end = 0 if reverse else c - 1
    b_end = b[end:end + 1, :]
    q_in = (q * jnp.exp(b)).astype(BF16)
    k_end = (k * jnp.exp(b_end - b)).astype(BF16)
    dec = jnp.exp(b_end)

    ti = lax.broadcasted_iota(jnp.int32, (c, c), 0)
    si = lax.broadcasted_iota(jnp.int32, (c, c), 1)
    tri = (si >= ti) if reverse else (si <= ti)

    scores, inter = [], []
    for hd in range(HG_HEADS):
        sl = slice(hd * HG_DIM, (hd + 1) * HG_DIM)
        scores.append(_dot_nt(jnp.concatenate([p[:, sl] for p in lhs], axis=1),
                              jnp.concatenate([p[:, sl] for p in rhs], axis=1)))
        st = st_ref[d, hd]
        inter.append(_dot_nt(q_in[:, sl], st.astype(BF16)))
        st_ref[d, hd] = st * dec[:, sl] + _dot_tn(v[:, sl], k_end[:, sl])

    def finish():
        outs = []
        for hd in range(HG_HEADS):
            sl = slice(hd * HG_DIM, (hd + 1) * HG_DIM)
            a = jnp.where(tri, scores[hd], 0.0).astype(BF16)
            outs.append(_dot(a, v[:, sl]) + inter[hd])
        o_ref[...] = jnp.concatenate(outs, axis=1).astype(o_ref.dtype)

    return finish


def _gla_kernel(qf_ref, vf_ref, zf_ref, qb_ref, vb_ref, zb_ref, lb_ref, of_ref, ob_ref, st_ref):
    @pl.when(pl.program_id(1) == 0)
    def _():
        st_ref[...] = jnp.zeros_like(st_ref)

    finish_f = _gla_direction(qf_ref, vf_ref, zf_ref, lb_ref[0:1, :], of_ref, st_ref, 0, False)
    finish_b = _gla_direction(qb_ref, vb_ref, zb_ref, lb_ref[1:2, :], ob_ref, st_ref, 1, True)
    finish_f()
    finish_b()


def gla_bidir(proj, z, lb, bsz, t):
    c = min(GLA_CHUNK, t)
    nch = t // c
    fwd = lambda col: pl.BlockSpec((c, HG_W), lambda b, j: (b * nch + j, col))
    bwd = lambda col: pl.BlockSpec((c, HG_W), lambda b, j: (b * nch + nch - 1 - j, col))
    out = jax.ShapeDtypeStruct((bsz * t, HG_W), BF16)
    return pl.pallas_call(
        _gla_kernel,
        grid=(bsz, nch),
        in_specs=[fwd(0), fwd(1), fwd(0), bwd(0), bwd(1), bwd(1), pl.BlockSpec((2, HG_W), lambda b, j: (0, 0))],
        out_specs=[fwd(0), bwd(0)],
        out_shape=[out, out],
        scratch_shapes=[pltpu.VMEM((2, HG_HEADS, HG_DIM, HG_DIM), F32)],
        compiler_params=_cparams("parallel", "arbitrary"),
        name="gla_bidir",
    )(proj, proj, z, proj, proj, z, lb)


def _conv_silu_kernel(x_ref, xp_ref, xn_ref, cw_ref, cb_ref, o_ref, *, tiles_per_seq):
    tm = x_ref.shape[0]
    width = x_ref.shape[1]
    n_ext = tm + 2 * HALO
    tpos = pl.program_id(0) % tiles_per_seq
    pad = SSM_CONV // 2
    slab_w = 2 * LANES
    for c0 in range(0, width, slab_w):
        sl = slice(c0, c0 + slab_w)
        prev = jnp.where(tpos == 0, 0.0, xp_ref[:, sl].astype(F32))
        nxt = jnp.where(tpos == tiles_per_seq - 1, 0.0, xn_ref[:, sl].astype(F32))
        ext = jnp.concatenate([prev, x_ref[:, sl].astype(F32), nxt], axis=0)
        acc = cb_ref[:, sl] + cw_ref[pad:pad + 1, sl] * ext[HALO:HALO + tm]
        for kk in range(SSM_CONV):
            if kk != pad:
                shifted = pltpu.roll(ext, (pad - kk) % n_ext, 0)
                acc = acc + cw_ref[kk:kk + 1, sl] * shifted[HALO:HALO + tm]
        o_ref[:, sl] = _silu(acc).astype(o_ref.dtype)


def conv_silu(proj, conv_w, conv_b, t, tm):
    n = proj.shape[0]
    w = SSM_CONV_DIM
    tiles_per_seq = t // tm
    hb = tm // HALO
    last_hb = n // HALO - 1
    return pl.pallas_call(
        functools.partial(_conv_silu_kernel, tiles_per_seq=tiles_per_seq),
        grid=(n // tm,),
        in_specs=[pl.BlockSpec((tm, w), lambda i: (i, 0)),
                  pl.BlockSpec((HALO, w), lambda i: (jnp.maximum(i * hb - 1, 0), 0)),
                  pl.BlockSpec((HALO, w), lambda i: (jnp.minimum((i + 1) * hb, last_hb), 0)),
                  _resident((SSM_CONV, w)), _resident((1, w))],
        out_specs=pl.BlockSpec((tm, w), lambda i: (i, 0)),
        out_shape=jax.ShapeDtypeStruct((n, w), BF16),
        compiler_params=_cparams("parallel"),
        name="conv_silu",
    )(proj, proj, proj, conv_w, conv_b)


def _expand_pair(cols, la, lb, rows):
    low = _half_mask((rows, LANES), 0)
    return jnp.where(low, jnp.broadcast_to(cols[:, la:la + 1], (rows, LANES)),
                     jnp.broadcast_to(cols[:, lb:lb + 1], (rows, LANES)))


def _ssd_direction(act_ref, dt_ref, dtb_ref, a_ref, dsk_ref, y_ref, st_ref, d, reverse):
    c = act_ref.shape[0]
    half = SSM_HEADS // 2
    base = d * SSM_HEADS
    xdt = dt_ref[...] + dtb_ref[...]
    dt = jnp.maximum(xdt, 0.0) + jnp.log(1.0 + jnp.exp(-jnp.abs(xdt)))
    cs = _scan_rows(dt * a_ref[...], reverse)
    end = 0 if reverse else c - 1
    cs_end = cs[end:end + 1, :]
    dec = jnp.exp(cs_end)

    def pair_rows(tab):
        tab_t = tab.T
        return jnp.concatenate([tab_t[base:base + half, :], tab_t[base + half:base + 2 * half, :]], axis=1)

    cs_rows = pair_rows(cs)
    dt_rows = pair_rows(dt)
    wd_rows = pair_rows(dt * jnp.exp(cs_end - cs))

    ti = lax.broadcasted_iota(jnp.int32, (c, LANES), 0)
    si = lax.broadcasted_iota(jnp.int32, (c, LANES), 1) % c
    tri = (si >= ti) if reverse else (si <= ti)
    low = _half_mask((c, LANES), 0)
    b_off = SSM_INNER
    c_off = SSM_INNER + SSM_GROUPS * SSM_STATE

    cbs, offs, bts = [], [], []
    for g in range(SSM_GROUPS):
        bm = act_ref[:, b_off + g * SSM_STATE:b_off + (g + 1) * SSM_STATE]
        cm = act_ref[:, c_off + g * SSM_STATE:c_off + (g + 1) * SSM_STATE]
        bm2 = jnp.concatenate([bm, bm], axis=0)
        cbs.append(_dot_nt(cm, bm2))
        bts.append(bm2.astype(F32).T)
        offs.append(_dot(cm, st_ref[d, g].astype(BF16)))

    def finish():
        for p in range(2 * SSM_GROUPS):
            g, pp = p // 2, p % 2
            sl = slice(p * LANES, (p + 1) * LANES)
            psl = slice(pp * LANES, (pp + 1) * LANES)
            xs = act_ref[:, sl]
            r = jnp.concatenate([jnp.where(low, xs, jnp.zeros_like(xs)),
                                 jnp.where(low, jnp.zeros_like(xs), xs)], axis=0)
            e = _expand_pair(cs, base + p, base + half + p, c)
            decay = jnp.exp(jnp.where(tri, e - cs_rows[p:p + 1, :], NEG_BIG))
            m = (cbs[g] * decay * dt_rows[p:p + 1, :]).astype(BF16)
            y = _dot(m, r) + offs[g][:, psl] * jnp.exp(e)
            if not reverse:
                y = y + dsk_ref[:, sl] * xs.astype(F32)
            y_ref[:, sl] = y.astype(y_ref.dtype)
            w = (bts[g] * wd_rows[p:p + 1, :]).astype(BF16)
            st_ref[d, g, :, psl] = (st_ref[d, g, :, psl] * _expand_pair(dec, base + p, base + half + p, 1)
                                    + _dot(w, r))

    return finish


def _ssd_kernel(af_ref, dtf_ref, ab_ref, dtb_in_ref, dtbias_ref, a_ref, dsk_ref, yf_ref, yb_ref, st_ref):
    @pl.when(pl.program_id(1) == 0)
    def _():
        st_ref[...] = jnp.zeros_like(st_ref)

    finish_f = _ssd_direction(af_ref, dtf_ref, dtbias_ref, a_ref, dsk_ref, yf_ref, st_ref, 0, False)
    finish_b = _ssd_direction(ab_ref, dtb_in_ref, dtbias_ref, a_ref, dsk_ref, yb_ref, st_ref, 1, True)
    finish_f()
    finish_b()


def ssd_bidir(act, dtp, dt_bias, a_neg, d_skip, bsz, t):
    c = min(SSD_CHUNK, t)
    nch = t // c
    w = SSM_CONV_DIM
    fwd = lambda width: pl.BlockSpec((c, width), lambda b, j: (b * nch + j, 0))
    bwd = lambda width: pl.BlockSpec((c, width), lambda b, j: (b * nch + nch - 1 - j, 0))
    out = jax.ShapeDtypeStruct((bsz * t, SSM_INNER), BF16)
    const = lambda shape: pl.BlockSpec(shape, lambda b, j: (0,) * len(shape))
    return pl.pallas_call(
        _ssd_kernel,
        grid=(bsz, nch),
        in_specs=[fwd(w), fwd(LANES), bwd(w), bwd(LANES), const((1, LANES)), const((1, LANES)),
                  const((1, SSM_INNER))],
        out_specs=[fwd(SSM_INNER), bwd(SSM_INNER)],
        out_shape=[out, out],
        scratch_shapes=[pltpu.VMEM((2, SSM_GROUPS, SSM_STATE, 4 * SSM_HEAD_DIM), F32)],
        compiler_params=_cparams("parallel", "arbitrary"),
        name="ssd_bidir",
    )(act, dtp, act, dtp, dt_bias, a_neg, d_skip)


def _hgrn_lower_bounds(logits):
    p = jax.nn.softmax(logits.astype(F32), axis=1)
    return jnp.cumsum(p, axis=1) - p[:, :1]


def _pad_lanes(v, width=LANES):
    return jnp.pad(v, (0, width - v.shape[0])).reshape(1, width)


def kernel(x, mem, norm_g, mem_norm_g, w_mem_kv, a_w_in, a_sink, a_w_out, b_w_in, b_lb_logits, b_norm_g, b_w_out,
           c_w_in, c_conv_w, c_conv_b, c_dt_bias, c_a_log, c_d, c_norm_g, c_w_out, d_w_in, d_rpb, d_w_out,
           ffn_w_in, ffn_conv_w, ffn_conv_b, ffn_w_out):
    bsz, t, _ = x.shape
    n = bsz * t
    tm = min(ROW_TILE, t)
    depth = norm_g.shape[0]
    h = x.reshape(n, D_MODEL)
    mem2 = mem.reshape(-1, D_MODEL)
    cos_t, sin_t = _rope_tables(t)
    lb = _hgrn_lower_bounds(b_lb_logits)
    bf = lambda w: w.astype(BF16)

    for layer in range(depth):
        kind, slot = layer % 4, layer // 4
        mem_kv = rms_matmul(mem2, mem_norm_g, bf(w_mem_kv[layer]), BF16, min(ROW_TILE, mem2.shape[0]))
        g0, g1, g2, g3 = (norm_g[layer, i] for i in range(4))

        if kind == 0 or kind == 3:
            w_in = a_w_in[slot] if kind == 0 else d_w_in[slot]
            w_out = a_w_out[slot] if kind == 0 else d_w_out[slot]
            proj = rms_matmul(h, g0, bf(w_in), BF16, tm)
            if kind == 0:
                o = attn_window(proj, a_sink[slot].astype(F32), cos_t, sin_t, bsz, t)
            else:
                o = attn_nbr(proj, d_rpb[slot], bsz, t)
            xo = cross_attn(proj, (ATT_Q_W + 2 * ATT_KV_W) // X_W, mem_kv, bsz, t)
            h = out_proj("plain", [(o, ATT_Q_W, 0)], xo, bf(w_out[:ATT_Q_W]), bf(w_out[ATT_Q_W:]), h, g1, tm)
        elif kind == 1:
            w_in = b_w_in[slot]
            w_main = jnp.concatenate([w_in[:, :2 * HG_W], w_in[:, 4 * HG_W:]], axis=1)
            proj = rms_matmul(h, g0, bf(w_main), BF16, tm)
            z = rms_matmul(h, g0, bf(w_in[:, 2 * HG_W:4 * HG_W]), F32, tm)
            lbs = jnp.stack([lb[0, layer], lb[1, layer]])
            of, ob = gla_bidir(proj, z, lbs, bsz, t)
            xo = cross_attn(proj, 3 * HG_W // X_W, mem_kv, bsz, t)
            w_out = b_w_out[slot]
            h = out_proj("hgrn", [(of, HG_W, 0), (ob, HG_W, 0), (proj, HG_W, 2), (b_norm_g[slot].reshape(1, HG_W), None, 0)],
                         xo, bf(w_out[:HG_W]), bf(w_out[HG_W:]), h, g1, tm)
        else:
            w_in = c_w_in[slot]
            z_w = w_in[:, :SSM_INNER]
            xbc_w = w_in[:, SSM_INNER:SSM_INNER + SSM_CONV_DIM]
            dt_w = w_in[:, SSM_INNER + SSM_CONV_DIM:SSM_INNER + SSM_CONV_DIM + 2 * SSM_HEADS]
            xq_w = w_in[:, SSM_INNER + SSM_CONV_DIM + 2 * SSM_HEADS:]
            proj = rms_matmul(h, g0, bf(jnp.concatenate([xbc_w, z_w, xq_w], axis=1)), BF16, tm)
            lane_src = np.array([dr * SSM_HEADS + 2 * p + par for dr in range(2) for par in range(2)
                                 for p in range(SSM_HEADS // 2)])
            dtp = rms_matmul(h, g0, bf(jnp.pad(dt_w[:, lane_src], ((0, 0), (0, LANES - 2 * SSM_HEADS)))), F32, tm)
            a_neg = -jnp.exp(c_a_log[slot].astype(F32)).reshape(-1)[lane_src]
            d_skip = jnp.repeat(c_d[slot].astype(F32), SSM_HEAD_DIM).reshape(1, SSM_INNER)
            act = conv_silu(proj, c_conv_w[slot].astype(F32), c_conv_b[slot].astype(F32).reshape(1, -1), t, tm)
            yf, yb = ssd_bidir(act, dtp, _pad_lanes(c_dt_bias[slot].astype(F32).reshape(-1)[lane_src]), _pad_lanes(a_neg),
                               d_skip, bsz, t)
            xo = cross_attn(proj, (SSM_CONV_DIM + SSM_INNER) // X_W, mem_kv, bsz, t)
            w_out = c_w_out[slot]
            h = out_proj("ssd", [(yf, SSM_INNER, 0), (yb, SSM_INNER, 0), (proj, SSM_INNER, SSM_CONV_DIM // SSM_INNER),
                                 (c_norm_g[slot].reshape(1, SSM_INNER), None, 0)],
                         xo, bf(w_out[:SSM_INNER]), bf(w_out[SSM_INNER:]), h, g1, tm)

        fw = ffn_w_in[layer]
        h = conv_ffn(h, g2, g3, bf(fw[:, :D_FF]), bf(fw[:, D_FF:]), ffn_conv_w[layer].astype(F32),
                     ffn_conv_b[layer].astype(F32), bf(ffn_w_out[layer]), t, tm)
    return h.reshape(bsz, t, D_MODEL)
```

```python
import functools
import math

import jax
import jax.numpy as jnp
import numpy as np
from jax import lax
from jax.experimental import pallas as pl
from jax.experimental.pallas import tpu as pltpu

F32 = jnp.float32
BF16 = jnp.bfloat16

D_MODEL = 1024
NORM_EPS = 1e-6
GRID_W = 64
ATT_HEADS = 16
ATT_KV_HEADS = 4
ATT_HEAD_DIM = 64
ATT_Q_W = 1024
ATT_KV_W = 256
ROPE_THETA = 10000.0
WINDOW = 128
NA_ROWS = 8
NA_COLS = 16
X_HEADS = 4
X_HEAD_DIM = 128
X_W = 512
HG_HEADS = 8
HG_DIM = 128
HG_W = 1024
SSM_INNER = 2048
SSM_HEADS = 32
SSM_HEAD_DIM = 64
SSM_GROUPS = 8
SSM_STATE = 128
SSM_CONV = 5
SSM_CONV_DIM = 4096
D_FF = 2816
DEPTH = 4

LANES = 128
BF16_SUBLANES = 16
VMEM_LIMIT_BYTES = 56 * 1024 * 1024

ROW_TILE = 512
FFN_ROW_TILE = 1024
COL_CHUNK = 512
FFN_CHUNK = 256
HALO = BF16_SUBLANES
GLA_CHUNK = 64
GLA_SUB = 16
SSD_CHUNK = LANES // 2
XATT_TILE = 512
NEG_BIG = -1e30


def _cparams(*sem):
    return pltpu.CompilerParams(dimension_semantics=sem, vmem_limit_bytes=VMEM_LIMIT_BYTES)


def _resident(shape):
    nd = len(shape)
    return pl.BlockSpec(shape, lambda *_: (0,) * nd, pipeline_mode=pl.Buffered(1))


def _rms_scale(x):
    return lax.rsqrt(jnp.mean(x * x, axis=-1, keepdims=True) + NORM_EPS)


def _sigmoid(x):
    return 0.5 * jnp.tanh(0.5 * x) + 0.5


def _silu(x):
    return x * _sigmoid(x)


def _dot(a, b):
    return jnp.dot(a, b, preferred_element_type=F32)


def _dot_nt(a, b):
    return lax.dot_general(a, b, (((1,), (1,)), ((), ())), preferred_element_type=F32)


def _dot_tn(a, b):
    return lax.dot_general(a, b, (((0,), (0,)), ((), ())), preferred_element_type=F32)


def _rms_matmul_kernel(x_ref, g_ref, w_ref, *o_refs, plan):
    x = x_ref[...]
    xn = (x * _rms_scale(x) * g_ref[...]).astype(BF16)
    for o_ref, ranges in zip(o_refs, plan):
        o0 = 0
        for start, stop in ranges:
            for c0 in range(start, stop, COL_CHUNK):
                c1 = min(c0 + COL_CHUNK, stop)
                o_ref[:, o0:o0 + c1 - c0] = _dot(xn, w_ref[:, c0:c1]).astype(o_ref.dtype)
                o0 += c1 - c0


def rms_matmul(x, g, w, outs, tm):
    m, k = x.shape
    widths = [sum(b - a for a, b in ranges) for _, ranges in outs]
    res = pl.pallas_call(
        functools.partial(_rms_matmul_kernel, plan=[ranges for _, ranges in outs]),
        grid=(m // tm,),
        in_specs=[pl.BlockSpec((tm, k), lambda i: (i, 0)),
                  _resident((1, k)),
                  _resident(w.shape)],
        out_specs=[pl.BlockSpec((tm, wd), lambda i: (i, 0)) for wd in widths],
        out_shape=[jax.ShapeDtypeStruct((m, wd), dt) for (dt, _), wd in zip(outs, widths)],
        compiler_params=_cparams("parallel"),
        name="rms_matmul",
    )(x, g.reshape(1, k), w)
    return res


def _rms_matmul_conv_kernel(x_ref, xp_ref, xn_ref, g_ref, w_ref, cw_ref, cb_ref, o_ref,
                            vn_ref, v_ref, vs_ref, pre_ref, res_ref, *, tiles_per_seq):
    tm = x_ref.shape[0]
    n_ext = tm + 2 * HALO
    nv = n_ext // 8
    tpos = pl.program_id(0) % tiles_per_seq
    pad = SSM_CONV // 2

    def normed(x):
        return x * _rms_scale(x) * g_ref[...]

    main = normed(x_ref[...])
    prev = jnp.where(tpos == 0, 0.0, normed(xp_ref[...]))
    nxt = jnp.where(tpos == tiles_per_seq - 1, 0.0, normed(xn_ref[...]))
    v_ref[...] = main.astype(BF16)
    n_lt = x_ref.shape[1] // LANES
    for lt in range(n_lt):
        lsl = slice(lt * LANES, (lt + 1) * LANES)
        vn_ref[lt, 0:HALO, :] = prev[:, lsl]
        vn_ref[lt, HALO:HALO + tm, :] = main[:, lsl]
        vn_ref[lt, HALO + tm:, :] = nxt[:, lsl]
    for j in range(0, nv, 2):
        pair = [jnp.concatenate([vn_ref[lt, pl.ds(j + dj, 8, stride=nv), :] for lt in range(n_lt)], axis=1)
                for dj in range(2)]
        vs_ref[8 * j:8 * j + 16, :] = jnp.concatenate(pair, axis=0).astype(BF16)

    chunk = 2 * LANES
    for ci, c0 in enumerate(range(0, SSM_CONV_DIM, chunk)):
        sl = slice(c0, c0 + chunk)
        pre = pre_ref.at[ci % 2]
        res = res_ref.at[ci % 2]
        pre[8 * pad:8 * pad + n_ext, :] = _dot(vs_ref[...], w_ref[:, sl])
        for i in range(pad):
            pre[8 * i:8 * i + 8, :] = pltpu.roll(pre[8 * (nv + i):8 * (nv + i) + 8, :], 1, 0)
            pre[8 * (pad + nv + i):8 * (pad + nv + i) + 8, :] = pltpu.roll(pre[8 * (pad + i):8 * (pad + i) + 8, :], 7, 0)
        acc = cb_ref[:, sl] + cw_ref[0:1, sl] * pre[0:n_ext, :]
        for kk in range(1, SSM_CONV):
            acc = acc + cw_ref[kk:kk + 1, sl] * pre[8 * kk:8 * kk + n_ext, :]
        act = _silu(acc)
        for l in range(chunk // LANES):
            for j in range(nv):
                res[l, pl.ds(j, 8, stride=nv), :] = act[8 * j:8 * j + 8, l * LANES:(l + 1) * LANES]
            o_ref[:, c0 + l * LANES:c0 + (l + 1) * LANES] = res[l, HALO:HALO + tm, :].astype(o_ref.dtype)
    width = o_ref.shape[1]
    for c0 in range(SSM_CONV_DIM, width, COL_CHUNK):
        c1 = min(c0 + COL_CHUNK, width)
        o_ref[:, c0:c1] = _dot(v_ref[...], w_ref[:, c0:c1]).astype(o_ref.dtype)


def rms_matmul_conv(x, g, w, conv_w, conv_b, t, tm):
    m, k = x.shape
    width = w.shape[1]
    tiles_per_seq = t // tm
    hb = tm // HALO
    last_hb = m // HALO - 1
    return pl.pallas_call(
        functools.partial(_rms_matmul_conv_kernel, tiles_per_seq=tiles_per_seq),
        grid=(m // tm,),
        in_specs=[pl.BlockSpec((tm, k), lambda i: (i, 0)),
                  pl.BlockSpec((HALO, k), lambda i: (jnp.maximum(i * hb - 1, 0), 0)),
                  pl.BlockSpec((HALO, k), lambda i: (jnp.minimum((i + 1) * hb, last_hb), 0)),
                  _resident((1, k)), _resident(w.shape), _resident(conv_w.shape), _resident(conv_b.shape)],
        out_specs=pl.BlockSpec((tm, width), lambda i: (i, 0)),
        out_shape=jax.ShapeDtypeStruct((m, width), BF16),
        scratch_shapes=[pltpu.VMEM((k // LANES, tm + 2 * HALO, LANES), F32),
                        pltpu.VMEM((tm, k), BF16),
                        pltpu.VMEM((tm + 2 * HALO, k), BF16),
                        pltpu.VMEM((2, tm + 2 * HALO + 16 * (SSM_CONV // 2), 2 * LANES), F32),
                        pltpu.VMEM((2, 2, tm + 2 * HALO, LANES), F32)],
        compiler_params=_cparams("parallel"),
        name="rms_matmul_conv",
    )(x, x, x, g.reshape(1, k), w, conv_w, conv_b)


def _xattn_kernel(q_ref, kv_ref, o_ref):
    scale = X_HEAD_DIM ** -0.5
    outs = []
    for h in range(X_HEADS):
        sl = slice(h * X_HEAD_DIM, (h + 1) * X_HEAD_DIM)
        q = q_ref[:, sl]
        k = kv_ref[:, sl]
        v = kv_ref[:, X_W + h * X_HEAD_DIM:X_W + (h + 1) * X_HEAD_DIM]
        s = _dot_nt(q, k) * scale
        p = jnp.exp(s - jnp.max(s, axis=-1, keepdims=True))
        l = jnp.sum(p, axis=-1, keepdims=True)
        outs.append(_dot(p.astype(BF16), v) / l)
    o_ref[...] = jnp.concatenate(outs, axis=1).astype(o_ref.dtype)


def cross_attn(proj, xq_block, mem_kv, kv_block, bsz, t):
    mem_len = mem_kv.shape[0] // bsz
    tq = min(XATT_TILE, t)
    nt = t // tq
    return pl.pallas_call(
        _xattn_kernel,
        grid=(bsz, nt),
        in_specs=[pl.BlockSpec((tq, X_W), lambda b, i: (b * nt + i, xq_block)),
                  pl.BlockSpec((mem_len, 2 * X_W), lambda b, i: (b, kv_block))],
        out_specs=pl.BlockSpec((tq, X_W), lambda b, i: (b * nt + i, 0)),
        out_shape=jax.ShapeDtypeStruct((bsz * t, X_W), BF16),
        compiler_params=_cparams("parallel", "parallel"),
        name="cross_attn",
    )(proj, mem_kv)


def _proj_norm_residual(o_bf16, xo_ref, w_ref, h_ref, g_ref, out_ref):
    k1 = o_bf16.shape[1]
    y = _dot(o_bf16, w_ref[0:k1, :]) + _dot(xo_ref[...], w_ref[k1:k1 + X_W, :])
    out_ref[...] = h_ref[...] + y * _rms_scale(y) * g_ref[...]


def _out_plain_kernel(o_ref, xo_ref, w_ref, h_ref, g_ref, out_ref):
    _proj_norm_residual(o_ref[...], xo_ref, w_ref, h_ref, g_ref, out_ref)


def _out_hgrn_kernel(of_ref, ob_ref, gate_ref, ng_ref, xo_ref, w_ref, h_ref, g_ref, out_ref):
    o = of_ref[...].astype(F32) + ob_ref[...].astype(F32)
    parts = []
    for hd in range(HG_HEADS):
        sl = slice(hd * HG_DIM, (hd + 1) * HG_DIM)
        oh = o[:, sl]
        on = oh * _rms_scale(oh) * ng_ref[:, sl]
        parts.append((on * _silu(gate_ref[:, sl].astype(F32))).astype(BF16))
    _proj_norm_residual(jnp.concatenate(parts, axis=1), xo_ref, w_ref, h_ref, g_ref, out_ref)


def _out_ssd_kernel(yf_ref, yb_ref, z_ref, ng_ref, xo_ref, w_ref, h_ref, g_ref, out_ref):
    gw = SSM_INNER // SSM_GROUPS
    parts = []
    for gi in range(SSM_GROUPS):
        sl = slice(gi * gw, (gi + 1) * gw)
        y = (yf_ref[:, sl].astype(F32) + yb_ref[:, sl].astype(F32)) * _silu(z_ref[:, sl].astype(F32))
        parts.append((y * _rms_scale(y) * ng_ref[:, sl]).astype(BF16))
    _proj_norm_residual(jnp.concatenate(parts, axis=1), xo_ref, w_ref, h_ref, g_ref, out_ref)


def _row_spec(tm, width, col_block=0):
    return pl.BlockSpec((tm, width), lambda i: (i, col_block))


def out_proj(kind, acts, xo, w, h, g, tm):
    n = h.shape[0]
    kernels = {"plain": _out_plain_kernel, "hgrn": _out_hgrn_kernel, "ssd": _out_ssd_kernel}
    in_specs, args = [], []
    for arr, width, cb in acts:
        if width is None:
            in_specs.append(_resident(arr.shape))
        else:
            in_specs.append(_row_spec(tm, width, cb))
        args.append(arr)
    in_specs += [_row_spec(tm, X_W), _resident(w.shape), _row_spec(tm, D_MODEL), _resident((1, D_MODEL))]
    args += [xo, w, h, g.reshape(1, D_MODEL)]
    return pl.pallas_call(
        kernels[kind],
        grid=(n // tm,),
        in_specs=in_specs,
        out_specs=_row_spec(tm, D_MODEL),
        out_shape=jax.ShapeDtypeStruct((n, D_MODEL), F32),
        compiler_params=_cparams("parallel"),
        name="out_proj_" + kind,
    )(*args)


def _gelu_tanh(x):
    c = math.sqrt(2.0 / math.pi)
    return x * (0.5 * (1.0 + jnp.tanh(c * (x + 0.044715 * (x * x * x)))))


def _ffn_kernel(h_ref, hp_ref, hn_ref, g2_ref, g3_ref, wi_ref, cw_ref, cb_ref, wo_ref, out_ref,
                v_ref, gate_ref, hid_ref, *, tiles_per_seq, n_chunks):
    tm = h_ref.shape[0]
    tpos = pl.program_id(0) % tiles_per_seq

    def normed(x):
        return (x * _rms_scale(x) * g2_ref[...]).astype(BF16)

    v_ref[HALO:HALO + tm, :] = normed(h_ref[...])
    v_ref[0:HALO, :] = jnp.where(tpos == 0, jnp.zeros((HALO, D_MODEL), BF16), normed(hp_ref[...]))
    v_ref[HALO + tm:, :] = jnp.where(tpos == tiles_per_seq - 1, jnp.zeros((HALO, D_MODEL), BF16),
                                     normed(hn_ref[...]))

    for c in range(n_chunks):
        sl = slice(c * FFN_CHUNK, (c + 1) * FFN_CHUNK)
        gate = gate_ref.at[c % 2]
        gate[...] = _dot(v_ref[...], wi_ref[:, sl])
        up = _dot(v_ref[HALO:HALO + tm, :], wi_ref[:, D_FF + c * FFN_CHUNK:D_FF + (c + 1) * FFN_CHUNK])
        conv = (cw_ref[0:1, sl] * gate[HALO - 1:HALO - 1 + tm, :]
                + cw_ref[1:2, sl] * gate[HALO:HALO + tm, :]
                + cw_ref[2:3, sl] * gate[HALO + 1:HALO + 1 + tm, :]
                + cb_ref[:, sl])
        hid_ref[:, sl] = (_gelu_tanh(conv) * up).astype(BF16)

    f = _dot(hid_ref[...], wo_ref[...])
    out_ref[...] = h_ref[...] + f * _rms_scale(f) * g3_ref[...]


def conv_ffn(h, g2, g3, wi, cw, cb, wo, t, tm):
    n = h.shape[0]
    tiles_per_seq = t // tm
    hb = tm // HALO
    last_hb = n // HALO - 1
    kern = functools.partial(_ffn_kernel, tiles_per_seq=tiles_per_seq, n_chunks=D_FF // FFN_CHUNK)
    return pl.pallas_call(
        kern,
        grid=(n // tm,),
        in_specs=[pl.BlockSpec((tm, D_MODEL), lambda i: (i, 0)),
                  pl.BlockSpec((HALO, D_MODEL), lambda i: (jnp.maximum(i * hb - 1, 0), 0)),
                  pl.BlockSpec((HALO, D_MODEL), lambda i: (jnp.minimum((i + 1) * hb, last_hb), 0)),
                  _resident((1, D_MODEL)), _resident((1, D_MODEL)),
                  _resident(wi.shape), _resident(cw.shape), _resident((1, D_FF)), _resident(wo.shape)],
        out_specs=pl.BlockSpec((tm, D_MODEL), lambda i: (i, 0)),
        out_shape=jax.ShapeDtypeStruct((n, D_MODEL), F32),
        scratch_shapes=[pltpu.VMEM((tm + 2 * HALO, D_MODEL), BF16),
                        pltpu.VMEM((2, tm + 2 * HALO, FFN_CHUNK), F32),
                        pltpu.VMEM((tm, D_FF), BF16)],
        compiler_params=_cparams("parallel"),
        name="conv_ffn",
    )(h, h, h, g2.reshape(1, D_MODEL), g3.reshape(1, D_MODEL), wi, cw, cb.reshape(1, D_FF), wo)


def _half_mask(shape, half):
    lane = lax.broadcasted_iota(jnp.int32, shape, len(shape) - 1)
    return (lane >= ATT_HEAD_DIM) if half else (lane < ATT_HEAD_DIM)


def _gqa_scores(qcols, g, kslab):
    rows = qcols[0].shape[0]
    kv_half = g % 2
    keep = _half_mask((rows, LANES), kv_half)
    qs = []
    for i in range(4):
        qc = qcols[2 * g + i // 2]
        if i % 2 != kv_half:
            qc = pltpu.roll(qc, ATT_HEAD_DIM, 1)
        qs.append(jnp.where(keep, qc, 0.0).astype(BF16))
    return _dot_nt(jnp.concatenate(qs, axis=0), kslab)


def _gqa_outputs(s, g, vslab, probs_fn):
    rows = s.shape[0] // 4
    kv_half = g % 2
    ps, denoms = [], []
    for i in range(4):
        p, den = probs_fn(i, s[i * rows:(i + 1) * rows])
        ps.append(p.astype(BF16))
        denoms.append(den)
    pv = _dot(jnp.concatenate(ps, axis=0), vslab)
    outs = []
    for i in range(4):
        o = pv[i * rows:(i + 1) * rows] / denoms[i]
        if i % 2 != kv_half:
            o = pltpu.roll(o, ATT_HEAD_DIM, 1)
        outs.append(o)
    low = _half_mask((rows, LANES), 0)
    return jnp.where(low, outs[0], outs[1]), jnp.where(low, outs[2], outs[3])


def _rope_cols(x, cos, sin_signed):
    lane = lax.broadcasted_iota(jnp.int32, x.shape, 1)
    first = (lane % ATT_HEAD_DIM) < (ATT_HEAD_DIM // 2)
    partner = jnp.where(first, pltpu.roll(x, LANES - ATT_HEAD_DIM // 2, 1), pltpu.roll(x, ATT_HEAD_DIM // 2, 1))
    return x * cos + partner * sin_signed


def _attn_window_kernel(sink_ref, q_ref, k_ref, v_ref, cos_ref, sin_ref, o_ref, kp_ref, vp_ref):
    t = k_ref.shape[0]
    blk = q_ref.shape[0]
    band = blk + 2 * WINDOW
    qb = pl.program_id(1)

    @pl.when(qb == 0)
    def _():
        zeros = jnp.zeros((WINDOW, ATT_KV_W), BF16)
        kp_ref[0:WINDOW, :] = zeros
        kp_ref[WINDOW + t:, :] = zeros
        vp_ref[0:WINDOW, :] = zeros
        vp_ref[WINDOW + t:, :] = zeros
        for j in range(ATT_KV_W // LANES):
            sl = slice(j * LANES, (j + 1) * LANES)
            kp_ref[WINDOW:WINDOW + t, sl] = _rope_cols(k_ref[:, sl].astype(F32), cos_ref[...], sin_ref[...]).astype(BF16)
        vp_ref[WINDOW:WINDOW + t, :] = v_ref[...]

    r0 = pl.multiple_of(qb * blk, blk)
    cos = cos_ref[pl.ds(r0, blk), :]
    sin = sin_ref[pl.ds(r0, blk), :]
    scale = ATT_HEAD_DIM ** -0.5
    qcols = [_rope_cols(q_ref[:, j * LANES:(j + 1) * LANES].astype(F32), cos, sin) * scale
             for j in range(ATT_Q_W // LANES)]

    ii = lax.broadcasted_iota(jnp.int32, (blk, band), 0)
    jj = lax.broadcasted_iota(jnp.int32, (blk, band), 1)
    kpos = r0 - WINDOW + jj
    valid = (jj - ii >= 0) & (jj - ii <= 2 * WINDOW) & (kpos >= 0) & (kpos < t)

    slabs = [slice((g // 2) * LANES, (g // 2 + 1) * LANES) for g in range(ATT_KV_HEADS)]
    scores = [_gqa_scores(qcols, g, kp_ref[pl.ds(r0, band), slabs[g]]) for g in range(ATT_KV_HEADS)]
    for g in range(ATT_KV_HEADS):
        vslab = vp_ref[pl.ds(r0, band), slabs[g]]

        def probs(i, s, g=g):
            sk = sink_ref[4 * g + i]
            s = jnp.where(valid, s, NEG_BIG)
            m = jnp.maximum(jnp.max(s, axis=-1, keepdims=True), sk)
            p = jnp.exp(s - m)
            return p, jnp.sum(p, axis=-1, keepdims=True) + jnp.exp(sk - m)

        o0, o1 = _gqa_outputs(scores[g], g, vslab, probs)
        o_ref[:, (2 * g) * LANES:(2 * g + 1) * LANES] = o0.astype(o_ref.dtype)
        o_ref[:, (2 * g + 1) * LANES:(2 * g + 2) * LANES] = o1.astype(o_ref.dtype)


def attn_window(proj, sink, cos_t, sin_t, bsz, t):
    blk = WINDOW
    nb = t // blk
    grid_spec = pltpu.PrefetchScalarGridSpec(
        num_scalar_prefetch=1,
        grid=(bsz, nb),
        in_specs=[pl.BlockSpec((blk, ATT_Q_W), lambda b, i, s: (b * nb + i, 0)),
                  pl.BlockSpec((t, ATT_KV_W), lambda b, i, s: (b, ATT_Q_W // ATT_KV_W)),
                  pl.BlockSpec((t, ATT_KV_W), lambda b, i, s: (b, ATT_Q_W // ATT_KV_W + 1)),
                  pl.BlockSpec((t, LANES), lambda b, i, s: (0, 0)),
                  pl.BlockSpec((t, LANES), lambda b, i, s: (0, 0))],
        out_specs=pl.BlockSpec((blk, ATT_Q_W), lambda b, i, s: (b * nb + i, 0)),
        scratch_shapes=[pltpu.VMEM((t + 2 * WINDOW, ATT_KV_W), BF16),
                        pltpu.VMEM((t + 2 * WINDOW, ATT_KV_W), BF16)],
    )
    return pl.pallas_call(
        _attn_window_kernel,
        grid_spec=grid_spec,
        out_shape=jax.ShapeDtypeStruct((bsz * t, ATT_Q_W), BF16),
        compiler_params=_cparams("parallel", "arbitrary"),
        name="attn_window",
    )(sink, proj, proj, proj, cos_t, sin_t)


def _rope_tables(t):
    half = ATT_HEAD_DIM // 2
    inv = ROPE_THETA ** (-jnp.arange(half, dtype=F32) / half)
    ang = jnp.arange(t).astype(F32)[:, None] * inv[None, :]
    cos = jnp.cos(ang)
    sin = jnp.sin(ang)
    reps = LANES // ATT_HEAD_DIM
    cos_t = jnp.tile(jnp.concatenate([cos, cos], axis=1), (1, reps))
    sin_t = jnp.tile(jnp.concatenate([-sin, sin], axis=1), (1, reps))
    return cos_t, sin_t


def _attn_nbr_kernel(q_ref, k_ref, v_ref, bias_ref, o_ref, *, rows, kr):
    r = pl.program_id(1)
    rs = jnp.clip(r - kr // 2, 0, rows - kr)
    k0 = pl.multiple_of(rs * GRID_W, GRID_W)
    nk = kr * GRID_W
    scale = ATT_HEAD_DIM ** -0.5
    qcols = [q_ref[:, j * LANES:(j + 1) * LANES].astype(F32) * scale for j in range(ATT_Q_W // LANES)]
    slabs = [slice((g // 2) * LANES, (g // 2 + 1) * LANES) for g in range(ATT_KV_HEADS)]
    scores = [_gqa_scores(qcols, g, k_ref[pl.ds(k0, nk), slabs[g]]) for g in range(ATT_KV_HEADS)]
    for g in range(ATT_KV_HEADS):
        vslab = v_ref[pl.ds(k0, nk), slabs[g]]

        def probs(i, s, g=g):
            s = s + bias_ref[0, 4 * g + i]
            p = jnp.exp(s - jnp.max(s, axis=-1, keepdims=True))
            return p, jnp.sum(p, axis=-1, keepdims=True)

        o0, o1 = _gqa_outputs(scores[g], g, vslab, probs)
        o_ref[:, (2 * g) * LANES:(2 * g + 1) * LANES] = o0.astype(o_ref.dtype)
        o_ref[:, (2 * g + 1) * LANES:(2 * g + 2) * LANES] = o1.astype(o_ref.dtype)


def _nbr_bias_tables(rpb, kr):
    qc = np.arange(GRID_W)
    kc = np.arange(GRID_W)
    col_start = np.clip(qc - NA_COLS // 2, 0, GRID_W - NA_COLS)
    in_win = (kc[None, :] >= col_start[:, None]) & (kc[None, :] < col_start[:, None] + NA_COLS)
    dc = np.clip(kc[None, :] - qc[:, None] + NA_COLS - 1, 0, 2 * NA_COLS - 2)
    pick = ((dc[:, :, None] == np.arange(2 * NA_COLS - 1)) & in_win[:, :, None]).astype(np.float32)
    by_off = jnp.stack([rpb.astype(F32)[:, NA_ROWS - 1 - off:NA_ROWS - 1 - off + kr] for off in range(kr)])
    tab = jnp.einsum('ohid,qkd->ohqik', by_off, pick, precision=lax.Precision.HIGHEST)
    tab = tab + jnp.where(in_win, 0.0, NEG_BIG).astype(F32)[None, None, :, None, :]
    return tab.reshape(kr, ATT_HEADS, GRID_W, kr * GRID_W)


def attn_nbr(proj, rpb, bsz, t):
    rows = t // GRID_W
    kr = min(NA_ROWS, rows)
    bias = _nbr_bias_tables(rpb, kr)

    def bias_index(b, r):
        return (r - jnp.clip(r - kr // 2, 0, rows - kr), 0, 0, 0)

    return pl.pallas_call(
        functools.partial(_attn_nbr_kernel, rows=rows, kr=kr),
        grid=(bsz, rows),
        in_specs=[pl.BlockSpec((GRID_W, ATT_Q_W), lambda b, r: (b * rows + r, 0)),
                  pl.BlockSpec((t, ATT_KV_W), lambda b, r: (b, ATT_Q_W // ATT_KV_W)),
                  pl.BlockSpec((t, ATT_KV_W), lambda b, r: (b, ATT_Q_W // ATT_KV_W + 1)),
                  pl.BlockSpec((1, ATT_HEADS, GRID_W, kr * GRID_W), bias_index)],
        out_specs=pl.BlockSpec((GRID_W, ATT_Q_W), lambda b, r: (b * rows + r, 0)),
        out_shape=jax.ShapeDtypeStruct((bsz * t, ATT_Q_W), BF16),
        compiler_params=_cparams("parallel", "arbitrary"),
        name="attn_nbr",
    )(proj, proj, proj, bias)


def _scan_rows(x, reverse):
    n = x.shape[0]
    row = lax.broadcasted_iota(jnp.int32, x.shape, 0)
    s = 1
    while s < n:
        if reverse:
            x = x + jnp.where(row < n - s, pltpu.roll(x, n - s, 0), 0.0)
        else:
            x = x + jnp.where(row >= s, pltpu.roll(x, s, 0), 0.0)
        s *= 2
    return x


def _cumsum_rows_mxu(x, reverse):
    n = x.shape[0]
    ti = lax.broadcasted_iota(jnp.int32, (n, n), 0)
    si = lax.broadcasted_iota(jnp.int32, (n, n), 1)
    ones = jnp.where((si >= ti) if reverse else (si <= ti), 1.0, 0.0).astype(BF16)
    total, rest = None, x
    for _ in range(3):
        term = rest.astype(BF16)
        part = _dot(ones, term)
        total = part if total is None else total + part
        rest = rest - term.astype(F32)
    return total


def _gla_gates(q_ref, z_ref, lb, reverse):
    hq = 0.5 * q_ref[...].astype(F32)
    q = hq * jnp.tanh(hq) + hq
    half_gap = 0.5 * (1.0 - lb)
    f = (lb + half_gap) + half_gap * jnp.tanh(0.5 * z_ref[...])
    return q, 1.0 - f, _cumsum_rows_mxu(jnp.log2(f), reverse)


def _gla_direction(gates, v_ref, o_ref, st_ref, d, reverse):
    q, k, b = gates
    c, width = q.shape
    nsub = c // GLA_SUB
    v = v_ref[...]

    def row(i):
        return b[i:i + 1, :]

    zero = jnp.zeros((1, width), F32)
    if reverse:
        near = [row((i + 1) * GLA_SUB) for i in range(nsub - 1)] + [zero]
        far = [row(i * GLA_SUB) for i in range(nsub)]
        b_end = far[0]
    else:
        near = [zero] + [row(i * GLA_SUB - 1) for i in range(1, nsub)]
        far = [row((i + 1) * GLA_SUB - 1) for i in range(nsub)]
        b_end = far[nsub - 1]

    def per_block(rows):
        return jnp.concatenate([jnp.broadcast_to(r, (GLA_SUB, width)) for r in rows], axis=0)

    q_near = q * jnp.exp2(b - per_block(near))
    k_far = k * jnp.exp2(per_block(far) - b)

    def blocks(x, scales, zero_blocks):
        pieces = []
        for i in range(nsub):
            if i in zero_blocks:
                pieces.append(jnp.zeros((GLA_SUB, width), BF16))
            else:
                pieces.append((x[i * GLA_SUB:(i + 1) * GLA_SUB] * scales[i]).astype(BF16))
        return jnp.concatenate(pieces, axis=0)

    lhs, rhs = [], []
    for j in range(nsub):
        seen_by = range(0, j + 1) if reverse else range(j, nsub)
        scales = {i: jnp.exp2(near[i] - far[j]) for i in seen_by}
        lhs.append(blocks(q_near, scales, [i for i in range(nsub) if i not in scales]))
        k_j = k_far[j * GLA_SUB:(j + 1) * GLA_SUB].astype(BF16)
        zeros = jnp.zeros((GLA_SUB, width), BF16)
        rhs.append(jnp.concatenate([k_j if i == j else zeros for i in range(nsub)], axis=0))

    q_in = blocks(q_near, [jnp.exp2(near[i]) for i in range(nsub)], [])
    k_end = blocks(k_far, [jnp.exp2(b_end - far[i]) for i in range(nsub)], [])
    dec = jnp.exp2(b_end)

    ti = lax.broadcasted_iota(jnp.int32, (c, c), 0)
    si = lax.broadcasted_iota(jnp.int32, (c, c), 1)
    tri = (si >= ti) if reverse else (si <= ti)

    scores, inter = [], []
    for hd in range(HG_HEADS):
        sl = slice(hd * HG_DIM, (hd + 1) * HG_DIM)
        scores.append(_dot_nt(jnp.concatenate([p[:, sl] for p in lhs], axis=1),
                              jnp.concatenate([p[:, sl] for p in rhs], axis=1)))
        st = st_ref[d, hd]
        inter.append(_dot_nt(q_in[:, sl], st.astype(BF16)))
        st_ref[d, hd] = st * dec[:, sl] + _dot_tn(v[:, sl], k_end[:, sl])

    def finish():
        outs = []
        for hd in range(HG_HEADS):
            sl = slice(hd * HG_DIM, (hd + 1) * HG_DIM)
            a = jnp.where(tri, scores[hd], 0.0).astype(BF16)
            outs.append(_dot(a, v[:, sl]) + inter[hd])
        o_ref[...] = jnp.concatenate(outs, axis=1).astype(o_ref.dtype)

    return finish


def _gla_kernel(qf_ref, vf_ref, zf_ref, qb_ref, vb_ref, zb_ref, lb_ref, of_ref, ob_ref, st_ref):
    @pl.when(pl.program_id(1) == 0)
    def _():
        st_ref[...] = jnp.zeros_like(st_ref)

    gates_f = _gla_gates(qf_ref, zf_ref, lb_ref[0:1, :], False)
    gates_b = _gla_gates(qb_ref, zb_ref, lb_ref[1:2, :], True)
    finish_f = _gla_direction(gates_f, vf_ref, of_ref, st_ref, 0, False)
    finish_b = _gla_direction(gates_b, vb_ref, ob_ref, st_ref, 1, True)
    finish_f()
    finish_b()


def gla_bidir(proj, z, lb, bsz, t):
    c = min(GLA_CHUNK, t)
    nch = t // c
    fwd = lambda col: pl.BlockSpec((c, HG_W), lambda b, j: (b * nch + j, col))
    bwd = lambda col: pl.BlockSpec((c, HG_W), lambda b, j: (b * nch + nch - 1 - j, col))
    out = jax.ShapeDtypeStruct((bsz * t, HG_W), BF16)
    return pl.pallas_call(
        _gla_kernel,
        grid=(bsz, nch),
        in_specs=[fwd(0), fwd(1), fwd(0), bwd(0), bwd(1), bwd(1), pl.BlockSpec((2, HG_W), lambda b, j: (0, 0))],
        out_specs=[fwd(0), bwd(0)],
        out_shape=[out, out],
        scratch_shapes=[pltpu.VMEM((2, HG_HEADS, HG_DIM, HG_DIM), F32)],
        compiler_params=_cparams("parallel", "arbitrary"),
        name="gla_bidir",
    )(proj, proj, z, proj, proj, z, lb)


def _expand_pair(cols, la, lb, rows):
    low = _half_mask((rows, LANES), 0)
    return jnp.where(low, jnp.broadcast_to(cols[:, la:la + 1], (rows, LANES)),
                     jnp.broadcast_to(cols[:, lb:lb + 1], (rows, LANES)))


def _ssd_direction(act_ref, dt_ref, dtb_ref, a_ref, dsk_ref, y_ref, st_ref, d, reverse):
    c = act_ref.shape[0]
    half = SSM_HEADS // 2
    base = d * SSM_HEADS
    xdt = dt_ref[...] + dtb_ref[...]
    dt = jnp.maximum(xdt, 0.0) + jnp.log(1.0 + jnp.exp(-jnp.abs(xdt)))
    cs = _scan_rows(dt * a_ref[...], reverse)
    end = 0 if reverse else c - 1
    cs_end = cs[end:end + 1, :]
    dec = jnp.exp(cs_end)

    def pair_rows(tab):
        tab_t = tab.T
        return jnp.concatenate([tab_t[base:base + half, :], tab_t[base + half:base + 2 * half, :]], axis=1)

    cs_rows = pair_rows(cs)
    dt_rows = pair_rows(dt)
    wd_rows = pair_rows(dt * jnp.exp(cs_end - cs))

    ti = lax.broadcasted_iota(jnp.int32, (c, LANES), 0)
    si = lax.broadcasted_iota(jnp.int32, (c, LANES), 1) % c
    tri = (si >= ti) if reverse else (si <= ti)
    low = _half_mask((c, LANES), 0)
    b_off = SSM_INNER
    c_off = SSM_INNER + SSM_GROUPS * SSM_STATE

    cbs, offs, bts = [], [], []
    for g in range(SSM_GROUPS):
        bm = act_ref[:, b_off + g * SSM_STATE:b_off + (g + 1) * SSM_STATE]
        cm = act_ref[:, c_off + g * SSM_STATE:c_off + (g + 1) * SSM_STATE]
        bm2 = jnp.concatenate([bm, bm], axis=0)
        cbs.append(_dot_nt(cm, bm2))
        bts.append(bm2.astype(F32).T)
        offs.append(_dot(cm, st_ref[d, g].astype(BF16)))

    def finish():
        for p in range(2 * SSM_GROUPS):
            g, pp = p // 2, p % 2
            sl = slice(p * LANES, (p + 1) * LANES)
            psl = slice(pp * LANES, (pp + 1) * LANES)
            xs = act_ref[:, sl]
            r = jnp.concatenate([jnp.where(low, xs, jnp.zeros_like(xs)),
                                 jnp.where(low, jnp.zeros_like(xs), xs)], axis=0)
            e = _expand_pair(cs, base + p, base + half + p, c)
            decay = jnp.exp(jnp.where(tri, e - cs_rows[p:p + 1, :], NEG_BIG))
            m = (cbs[g] * decay * dt_rows[p:p + 1, :]).astype(BF16)
            y = _dot(m, r) + offs[g][:, psl] * jnp.exp(e)
            if not reverse:
                y = y + dsk_ref[:, sl] * xs.astype(F32)
            y_ref[:, sl] = y.astype(y_ref.dtype)
            w = (bts[g] * wd_rows[p:p + 1, :]).astype(BF16)
            st_ref[d, g, :, psl] = (st_ref[d, g, :, psl] * _expand_pair(dec, base + p, base + half + p, 1)
                                    + _dot(w, r))

    return finish


def _ssd_kernel(af_ref, dtf_ref, ab_ref, dtb_in_ref, dtbias_ref, a_ref, dsk_ref, yf_ref, yb_ref, st_ref):
    @pl.when(pl.program_id(1) == 0)
    def _():
        st_ref[...] = jnp.zeros_like(st_ref)

    finish_f = _ssd_direction(af_ref, dtf_ref, dtbias_ref, a_ref, dsk_ref, yf_ref, st_ref, 0, False)
    finish_b = _ssd_direction(ab_ref, dtb_in_ref, dtbias_ref, a_ref, dsk_ref, yb_ref, st_ref, 1, True)
    finish_f()
    finish_b()


def ssd_bidir(act, dtp, dt_bias, a_neg, d_skip, bsz, t):
    c = min(SSD_CHUNK, t)
    nch = t // c
    w = SSM_CONV_DIM
    fwd = lambda width: pl.BlockSpec((c, width), lambda b, j: (b * nch + j, 0))
    bwd = lambda width: pl.BlockSpec((c, width), lambda b, j: (b * nch + nch - 1 - j, 0))
    out = jax.ShapeDtypeStruct((bsz * t, SSM_INNER), BF16)
    const = lambda shape: pl.BlockSpec(shape, lambda b, j: (0,) * len(shape))
    return pl.pallas_call(
        _ssd_kernel,
        grid=(bsz, nch),
        in_specs=[fwd(w), fwd(LANES), bwd(w), bwd(LANES), const((1, LANES)), const((1, LANES)),
                  const((1, SSM_INNER))],
        out_specs=[fwd(SSM_INNER), bwd(SSM_INNER)],
        out_shape=[out, out],
        scratch_shapes=[pltpu.VMEM((2, SSM_GROUPS, SSM_STATE, 4 * SSM_HEAD_DIM), F32)],
        compiler_params=_cparams("parallel", "arbitrary"),
        name="ssd_bidir",
    )(act, dtp, act, dtp, dt_bias, a_neg, d_skip)


def _hgrn_lower_bounds(logits):
    p = jax.nn.softmax(logits.astype(F32), axis=1)
    return jnp.cumsum(p, axis=1) - p[:, :1]


def _pad_lanes(v, width=LANES):
    return jnp.pad(v, (0, width - v.shape[0])).reshape(1, width)


def kernel(x, mem, norm_g, mem_norm_g, w_mem_kv, a_w_in, a_sink, a_w_out, b_w_in, b_lb_logits, b_norm_g, b_w_out,
           c_w_in, c_conv_w, c_conv_b, c_dt_bias, c_a_log, c_d, c_norm_g, c_w_out, d_w_in, d_rpb, d_w_out,
           ffn_w_in, ffn_conv_w, ffn_conv_b, ffn_w_out):
    bsz, t, _ = x.shape
    n = bsz * t
    tm = min(ROW_TILE, t)
    depth = norm_g.shape[0]
    h = x.reshape(n, D_MODEL)
    mem2 = mem.reshape(-1, D_MODEL)
    cos_t, sin_t = _rope_tables(t)
    lb = _hgrn_lower_bounds(b_lb_logits)
    bf = lambda w: w.astype(BF16)

    kv_w = bf(jnp.transpose(w_mem_kv, (1, 0, 2)).reshape(D_MODEL, depth * 2 * X_W))
    (mem_kv,) = rms_matmul(mem2, mem_norm_g, kv_w, [(BF16, [(0, depth * 2 * X_W)])], min(ROW_TILE, mem2.shape[0]))

    for layer in range(depth):
        kind, slot = layer % 4, layer // 4
        g0, g1, g2, g3 = (norm_g[layer, i] for i in range(4))

        if kind == 0 or kind == 3:
            w_in = a_w_in[slot] if kind == 0 else d_w_in[slot]
            w_out = a_w_out[slot] if kind == 0 else d_w_out[slot]
            (proj,) = rms_matmul(h, g0, bf(w_in), [(BF16, [(0, w_in.shape[1])])], tm)
            if kind == 0:
                o = attn_window(proj, a_sink[slot].astype(F32), cos_t, sin_t, bsz, t)
            else:
                o = attn_nbr(proj, d_rpb[slot], bsz, t)
            xo = cross_attn(proj, (ATT_Q_W + 2 * ATT_KV_W) // X_W, mem_kv, layer, bsz, t)
            h = out_proj("plain", [(o, ATT_Q_W, 0)], xo, bf(w_out), h, g1, tm)
        elif kind == 1:
            proj, z = rms_matmul(h, g0, bf(b_w_in[slot]),
                                 [(BF16, [(0, 2 * HG_W), (4 * HG_W, 5 * HG_W + X_W)]), (F32, [(2 * HG_W, 4 * HG_W)])], tm)
            lbs = jnp.stack([lb[0, layer], lb[1, layer]])
            of, ob = gla_bidir(proj, z, lbs, bsz, t)
            xo = cross_attn(proj, 3 * HG_W // X_W, mem_kv, layer, bsz, t)
            h = out_proj("hgrn", [(of, HG_W, 0), (ob, HG_W, 0), (proj, HG_W, 2), (b_norm_g[slot].reshape(1, HG_W), None, 0)],
                         xo, bf(b_w_out[slot]), h, g1, tm)
        else:
            w_in = c_w_in[slot]
            z_w = w_in[:, :SSM_INNER]
            xbc_w = w_in[:, SSM_INNER:SSM_INNER + SSM_CONV_DIM]
            dt_w = w_in[:, SSM_INNER + SSM_CONV_DIM:SSM_INNER + SSM_CONV_DIM + 2 * SSM_HEADS]
            xq_w = w_in[:, SSM_INNER + SSM_CONV_DIM + 2 * SSM_HEADS:]
            proj = rms_matmul_conv(h, g0, bf(jnp.concatenate([xbc_w, z_w, xq_w], axis=1)), c_conv_w[slot].astype(F32),
                                   c_conv_b[slot].astype(F32).reshape(1, -1), t, tm)
            lane_src = np.array([dr * SSM_HEADS + 2 * p + par for dr in range(2) for par in range(2)
                                 for p in range(SSM_HEADS // 2)])
            (dtp,) = rms_matmul(h, g0, bf(jnp.pad(dt_w[:, lane_src], ((0, 0), (0, LANES - 2 * SSM_HEADS)))),
                                [(F32, [(0, LANES)])], tm)
            a_neg = -jnp.exp(c_a_log[slot].astype(F32)).reshape(-1)[lane_src]
            d_skip = jnp.repeat(c_d[slot].astype(F32), SSM_HEAD_DIM).reshape(1, SSM_INNER)
            yf, yb = ssd_bidir(proj, dtp, _pad_lanes(c_dt_bias[slot].astype(F32).reshape(-1)[lane_src]), _pad_lanes(a_neg),
                               d_skip, bsz, t)
            xo = cross_attn(proj, (SSM_CONV_DIM + SSM_INNER) // X_W, mem_kv, layer, bsz, t)
            h = out_proj("ssd", [(yf, SSM_INNER, 0), (yb, SSM_INNER, 0), (proj, SSM_INNER, SSM_CONV_DIM // SSM_INNER),
                                 (c_norm_g[slot].reshape(1, SSM_INNER), None, 0)],
                         xo, bf(c_w_out[slot]), h, g1, tm)

        h = conv_ffn(h, g2, g3, bf(ffn_w_in[layer]), ffn_conv_w[layer].astype(F32),
                     ffn_conv_b[layer].astype(F32), bf(ffn_w_out[layer]), t, min(FFN_ROW_TILE, t))
    return h.reshape(bsz, t, D_MODEL)
```

```python
import functools
import math

import jax
import jax.numpy as jnp
import numpy as np
from jax import lax
from jax.experimental import pallas as pl
from jax.experimental.pallas import tpu as pltpu

F32 = jnp.float32
BF16 = jnp.bfloat16

D_MODEL = 1024
NORM_EPS = 1e-6
GRID_W = 64
ATT_HEADS = 16
ATT_KV_HEADS = 4
ATT_HEAD_DIM = 64
ATT_Q_W = 1024
ATT_KV_W = 256
ROPE_THETA = 10000.0
WINDOW = 128
NA_ROWS = 8
NA_COLS = 16
NA_QCOLS = 16
NA_KCOLS = 32
NBR_STEP_ROWS = 2
WIN_STEP_BLOCKS = 2
X_HEADS = 4
X_HEAD_DIM = 128
X_W = 512
HG_HEADS = 8
HG_DIM = 128
HG_W = 1024
SSM_INNER = 2048
SSM_HEADS = 32
SSM_HEAD_DIM = 64
SSM_GROUPS = 8
SSM_STATE = 128
SSM_CONV = 5
SSM_CONV_DIM = 4096
D_FF = 2816
DEPTH = 4

LANES = 128
BF16_SUBLANES = 16
VMEM_LIMIT_BYTES = 56 * 1024 * 1024

ROW_TILE = 512
FFN_ROW_TILE = 1024
FFN_SPLIT = 2
COL_CHUNK = 512
FFN_CHUNK = 256
HALO = BF16_SUBLANES
GLA_CHUNK = 64
GLA_STEP_CHUNKS = 2
GLA_SUB = 16
SSD_CHUNK = LANES // 2
SSD_STEP_CHUNKS = 2
XATT_TILE = 512
NEG_BIG = -1e30


def _cparams(*sem):
    return pltpu.CompilerParams(dimension_semantics=sem, vmem_limit_bytes=VMEM_LIMIT_BYTES)


def _resident(shape):
    nd = len(shape)
    return pl.BlockSpec(shape, lambda *_: (0,) * nd, pipeline_mode=pl.Buffered(1))


def _rms_scale(x):
    return lax.rsqrt(jnp.mean(x * x, axis=-1, keepdims=True) + NORM_EPS)


def _sigmoid(x):
    return 0.5 * jnp.tanh(0.5 * x) + 0.5


def _silu(x):
    return x * _sigmoid(x)


def _dot(a, b):
    return jnp.dot(a, b, preferred_element_type=F32)


def _dot_nt(a, b):
    return lax.dot_general(a, b, (((1,), (1,)), ((), ())), preferred_element_type=F32)


def _dot_tn(a, b):
    return lax.dot_general(a, b, (((0,), (0,)), ((), ())), preferred_element_type=F32)


def _rms_matmul_kernel(x_ref, g_ref, w_ref, *o_refs, plan):
    x = x_ref[...]
    xn = (x * _rms_scale(x) * g_ref[...]).astype(BF16)
    for o_ref, ranges in zip(o_refs, plan):
        o0 = 0
        for start, stop in ranges:
            for c0 in range(start, stop, COL_CHUNK):
                c1 = min(c0 + COL_CHUNK, stop)
                o_ref[:, o0:o0 + c1 - c0] = _dot(xn, w_ref[:, c0:c1]).astype(o_ref.dtype)
                o0 += c1 - c0


def rms_matmul(x, g, w, outs, tm):
    m, k = x.shape
    widths = [sum(b - a for a, b in ranges) for _, ranges in outs]
    res = pl.pallas_call(
        functools.partial(_rms_matmul_kernel, plan=[ranges for _, ranges in outs]),
        grid=(m // tm,),
        in_specs=[pl.BlockSpec((tm, k), lambda i: (i, 0)),
                  _resident((1, k)),
                  _resident(w.shape)],
        out_specs=[pl.BlockSpec((tm, wd), lambda i: (i, 0)) for wd in widths],
        out_shape=[jax.ShapeDtypeStruct((m, wd), dt) for (dt, _), wd in zip(outs, widths)],
        compiler_params=_cparams("parallel"),
        name="rms_matmul",
    )(x, g.reshape(1, k), w)
    return res


def _rms_matmul_conv_kernel(x_ref, xp_ref, xn_ref, g_ref, w_ref, cw_ref, cb_ref, o_ref, tail_ref,
                            vn_ref, v_ref, vs_ref, pre_ref, res_ref, *, tiles_per_seq):
    tm = x_ref.shape[0]
    n_ext = tm + 2 * HALO
    nv = n_ext // 8
    tpos = pl.program_id(0) % tiles_per_seq
    pad = SSM_CONV // 2

    def normed(x):
        return x * _rms_scale(x) * g_ref[...]

    main = normed(x_ref[...])
    prev = jnp.where(tpos == 0, 0.0, normed(xp_ref[...]))
    nxt = jnp.where(tpos == tiles_per_seq - 1, 0.0, normed(xn_ref[...]))
    v_ref[...] = main.astype(BF16)
    n_lt = x_ref.shape[1] // LANES
    for lt in range(n_lt):
        lsl = slice(lt * LANES, (lt + 1) * LANES)
        vn_ref[lt, 0:HALO, :] = prev[:, lsl]
        vn_ref[lt, HALO:HALO + tm, :] = main[:, lsl]
        vn_ref[lt, HALO + tm:, :] = nxt[:, lsl]
    for j in range(0, nv, 2):
        pair = [jnp.concatenate([vn_ref[lt, pl.ds(j + dj, 8, stride=nv), :] for lt in range(n_lt)], axis=1)
                for dj in range(2)]
        vs_ref[8 * j:8 * j + 16, :] = jnp.concatenate(pair, axis=0).astype(BF16)

    chunk = 2 * LANES
    for ci, c0 in enumerate(range(0, SSM_CONV_DIM, chunk)):
        sl = slice(c0, c0 + chunk)
        pre = pre_ref.at[ci % 2]
        res = res_ref.at[ci % 2]
        pre[8 * pad:8 * pad + n_ext, :] = _dot(vs_ref[...], w_ref[:, sl])
        for i in range(pad):
            pre[8 * i:8 * i + 8, :] = pltpu.roll(pre[8 * (nv + i):8 * (nv + i) + 8, :], 1, 0)
            pre[8 * (pad + nv + i):8 * (pad + nv + i) + 8, :] = pltpu.roll(pre[8 * (pad + i):8 * (pad + i) + 8, :], 7, 0)
        acc = cb_ref[:, sl] + cw_ref[0:1, sl] * pre[0:n_ext, :]
        for kk in range(1, SSM_CONV):
            acc = acc + cw_ref[kk:kk + 1, sl] * pre[8 * kk:8 * kk + n_ext, :]
        act = _silu(acc)
        for l in range(chunk // LANES):
            for j in range(nv):
                res[l, pl.ds(j, 8, stride=nv), :] = act[8 * j:8 * j + 8, l * LANES:(l + 1) * LANES]
            o_ref[:, c0 + l * LANES:c0 + (l + 1) * LANES] = res[l, HALO:HALO + tm, :].astype(o_ref.dtype)
    width = o_ref.shape[1]
    for c0 in range(SSM_CONV_DIM, width, COL_CHUNK):
        c1 = min(c0 + COL_CHUNK, width)
        o_ref[:, c0:c1] = _dot(v_ref[...], w_ref[:, c0:c1]).astype(o_ref.dtype)
    tail_ref[...] = _dot(v_ref[...], w_ref[:, width:])


def rms_matmul_conv(x, g, w, conv_w, conv_b, t, tm):
    m, k = x.shape
    width = w.shape[1] - LANES
    tiles_per_seq = t // tm
    hb = tm // HALO
    last_hb = m // HALO - 1
    return pl.pallas_call(
        functools.partial(_rms_matmul_conv_kernel, tiles_per_seq=tiles_per_seq),
        grid=(m // tm,),
        in_specs=[pl.BlockSpec((tm, k), lambda i: (i, 0)),
                  pl.BlockSpec((HALO, k), lambda i: (jnp.maximum(i * hb - 1, 0), 0)),
                  pl.BlockSpec((HALO, k), lambda i: (jnp.minimum((i + 1) * hb, last_hb), 0)),
                  _resident((1, k)), _resident(w.shape), _resident(conv_w.shape), _resident(conv_b.shape)],
        out_specs=[pl.BlockSpec((tm, width), lambda i: (i, 0)), pl.BlockSpec((tm, LANES), lambda i: (i, 0))],
        out_shape=[jax.ShapeDtypeStruct((m, width), BF16), jax.ShapeDtypeStruct((m, LANES), F32)],
        scratch_shapes=[pltpu.VMEM((k // LANES, tm + 2 * HALO, LANES), F32),
                        pltpu.VMEM((tm, k), BF16),
                        pltpu.VMEM((tm + 2 * HALO, k), BF16),
                        pltpu.VMEM((2, tm + 2 * HALO + 16 * (SSM_CONV // 2), 2 * LANES), F32),
                        pltpu.VMEM((2, 2, tm + 2 * HALO, LANES), F32)],
        compiler_params=_cparams("parallel"),
        name="rms_matmul_conv",
    )(x, x, x, g.reshape(1, k), w, conv_w, conv_b)


def _xattn_kernel(q_ref, kv_ref, o_ref):
    scale = X_HEAD_DIM ** -0.5
    outs = []
    for h in range(X_HEADS):
        sl = slice(h * X_HEAD_DIM, (h + 1) * X_HEAD_DIM)
        q = q_ref[:, sl]
        k = kv_ref[:, sl]
        v = kv_ref[:, X_W + h * X_HEAD_DIM:X_W + (h + 1) * X_HEAD_DIM]
        s = _dot_nt(q, k) * scale
        p = jnp.exp(s - jnp.max(s, axis=-1, keepdims=True))
        l = jnp.sum(p, axis=-1, keepdims=True)
        outs.append(_dot(p.astype(BF16), v) / l)
    o_ref[...] = jnp.concatenate(outs, axis=1).astype(o_ref.dtype)


def cross_attn(proj, xq_block, mem_kv, kv_block, bsz, t):
    mem_len = mem_kv.shape[0] // bsz
    tq = min(XATT_TILE, t)
    nt = t // tq
    return pl.pallas_call(
        _xattn_kernel,
        grid=(bsz, nt),
        in_specs=[pl.BlockSpec((tq, X_W), lambda b, i: (b * nt + i, xq_block)),
                  pl.BlockSpec((mem_len, 2 * X_W), lambda b, i: (b, kv_block))],
        out_specs=pl.BlockSpec((tq, X_W), lambda b, i: (b * nt + i, 0)),
        out_shape=jax.ShapeDtypeStruct((bsz * t, X_W), BF16),
        compiler_params=_cparams("parallel", "parallel"),
        name="cross_attn",
    )(proj, mem_kv)


def _proj_norm_residual(o_bf16, xo_ref, w_ref, h_ref, g_ref, out_ref):
    k1 = o_bf16.shape[1]
    y = _dot(o_bf16, w_ref[0:k1, :]) + _dot(xo_ref[...], w_ref[k1:k1 + X_W, :])
    out_ref[...] = h_ref[...] + y * _rms_scale(y) * g_ref[...]


def _out_plain_kernel(o_ref, xo_ref, w_ref, h_ref, g_ref, out_ref):
    _proj_norm_residual(o_ref[...], xo_ref, w_ref, h_ref, g_ref, out_ref)


def _out_hgrn_kernel(of_ref, ob_ref, gate_ref, ng_ref, xo_ref, w_ref, h_ref, g_ref, out_ref):
    o = of_ref[...].astype(F32) + ob_ref[...].astype(F32)
    parts = []
    for hd in range(HG_HEADS):
        sl = slice(hd * HG_DIM, (hd + 1) * HG_DIM)
        oh = o[:, sl]
        on = oh * _rms_scale(oh) * ng_ref[:, sl]
        parts.append((on * _silu(gate_ref[:, sl].astype(F32))).astype(BF16))
    _proj_norm_residual(jnp.concatenate(parts, axis=1), xo_ref, w_ref, h_ref, g_ref, out_ref)


def _out_ssd_kernel(yf_ref, yb_ref, z_ref, ng_ref, xo_ref, w_ref, h_ref, g_ref, out_ref):
    gw = SSM_INNER // SSM_GROUPS
    parts = []
    for gi in range(SSM_GROUPS):
        sl = slice(gi * gw, (gi + 1) * gw)
        y = (yf_ref[:, sl].astype(F32) + yb_ref[:, sl].astype(F32)) * _silu(z_ref[:, sl].astype(F32))
        parts.append((y * _rms_scale(y) * ng_ref[:, sl]).astype(BF16))
    _proj_norm_residual(jnp.concatenate(parts, axis=1), xo_ref, w_ref, h_ref, g_ref, out_ref)


def _row_spec(tm, width, col_block=0):
    return pl.BlockSpec((tm, width), lambda i: (i, col_block))


def out_proj(kind, acts, xo, w, h, g, tm):
    n = h.shape[0]
    kernels = {"plain": _out_plain_kernel, "hgrn": _out_hgrn_kernel, "ssd": _out_ssd_kernel}
    in_specs, args = [], []
    for arr, width, cb in acts:
        if width is None:
            in_specs.append(_resident(arr.shape))
        else:
            in_specs.append(_row_spec(tm, width, cb))
        args.append(arr)
    in_specs += [_row_spec(tm, X_W), _resident(w.shape), _row_spec(tm, D_MODEL), _resident((1, D_MODEL))]
    args += [xo, w, h, g.reshape(1, D_MODEL)]
    return pl.pallas_call(
        kernels[kind],
        grid=(n // tm,),
        in_specs=in_specs,
        out_specs=_row_spec(tm, D_MODEL),
        out_shape=jax.ShapeDtypeStruct((n, D_MODEL), F32),
        compiler_params=_cparams("parallel"),
        name="out_proj_" + kind,
    )(*args)


def _gelu_tanh(x):
    c = math.sqrt(2.0 / math.pi)
    return x * (0.5 * (1.0 + jnp.tanh(c * (x + 0.044715 * (x * x * x)))))


def _ffn_kernel(h_ref, hp_ref, hn_ref, g2_ref, g3_ref, wi_ref, cw_ref, cb_ref, wo_ref, out_ref,
                v_ref, gate_ref, hid_ref, *, tiles_per_seq, n_chunks):
    tm = h_ref.shape[0]
    tpos = pl.program_id(0) % tiles_per_seq

    def normed(x):
        return (x * _rms_scale(x) * g2_ref[...]).astype(BF16)

    hm = tm // FFN_SPLIT
    v_ref[0:HALO, :] = jnp.where(tpos == 0, jnp.zeros((HALO, D_MODEL), BF16), normed(hp_ref[...]))
    for part in range(FFN_SPLIT):
        v_ref[HALO + part * hm:HALO + (part + 1) * hm, :] = normed(h_ref[part * hm:(part + 1) * hm, :])
    v_ref[HALO + tm:, :] = jnp.where(tpos == tiles_per_seq - 1, jnp.zeros((HALO, D_MODEL), BF16),
                                     normed(hn_ref[...]))

    for part in range(FFN_SPLIT):
        base = part * hm
        for c in range(n_chunks):
            sl = slice(c * FFN_CHUNK, (c + 1) * FFN_CHUNK)
            gate = gate_ref.at[(part * n_chunks + c) % 2]
            gate[...] = _dot(v_ref[base:base + hm + 2 * HALO, :], wi_ref[:, sl])
            up = _dot(v_ref[base + HALO:base + HALO + hm, :],
                      wi_ref[:, D_FF + c * FFN_CHUNK:D_FF + (c + 1) * FFN_CHUNK])
            conv = (cw_ref[0:1, sl] * gate[HALO - 1:HALO - 1 + hm, :]
                    + cw_ref[1:2, sl] * gate[HALO:HALO + hm, :]
                    + cw_ref[2:3, sl] * gate[HALO + 1:HALO + 1 + hm, :]
                    + cb_ref[:, sl])
            hid_ref[base:base + hm, sl] = (_gelu_tanh(conv) * up).astype(BF16)
        f = _dot(hid_ref[base:base + hm, :], wo_ref[...])
        out_ref[base:base + hm, :] = h_ref[base:base + hm, :] + f * _rms_scale(f) * g3_ref[...]


def conv_ffn(h, g2, g3, wi, cw, cb, wo, t, tm):
    n = h.shape[0]
    tiles_per_seq = t // tm
    hb = tm // HALO
    last_hb = n // HALO - 1
    kern = functools.partial(_ffn_kernel, tiles_per_seq=tiles_per_seq, n_chunks=D_FF // FFN_CHUNK)
    return pl.pallas_call(
        kern,
        grid=(n // tm,),
        in_specs=[pl.BlockSpec((tm, D_MODEL), lambda i: (i, 0)),
                  pl.BlockSpec((HALO, D_MODEL), lambda i: (jnp.maximum(i * hb - 1, 0), 0)),
                  pl.BlockSpec((HALO, D_MODEL), lambda i: (jnp.minimum((i + 1) * hb, last_hb), 0)),
                  _resident((1, D_MODEL)), _resident((1, D_MODEL)),
                  _resident(wi.shape), _resident(cw.shape), _resident((1, D_FF)), _resident(wo.shape)],
        out_specs=pl.BlockSpec((tm, D_MODEL), lambda i: (i, 0)),
        out_shape=jax.ShapeDtypeStruct((n, D_MODEL), F32),
        scratch_shapes=[pltpu.VMEM((tm + 2 * HALO, D_MODEL), BF16),
                        pltpu.VMEM((2, tm // FFN_SPLIT + 2 * HALO, FFN_CHUNK), F32),
                        pltpu.VMEM((tm, D_FF), BF16)],
        compiler_params=_cparams("parallel"),
        name="conv_ffn",
    )(h, h, h, g2.reshape(1, D_MODEL), g3.reshape(1, D_MODEL), wi, cw, cb.reshape(1, D_FF), wo)


def _half_mask(shape, half):
    lane = lax.broadcasted_iota(jnp.int32, shape, len(shape) - 1)
    return (lane >= ATT_HEAD_DIM) if half else (lane < ATT_HEAD_DIM)


def _gqa_scores(qcols, g, kslab):
    rows = qcols[0].shape[0]
    kv_half = g % 2
    keep = _half_mask((rows, LANES), kv_half)
    qs = []
    for i in range(4):
        qc = qcols[2 * g + i // 2]
        if i % 2 != kv_half:
            qc = pltpu.roll(qc, ATT_HEAD_DIM, 1)
        qs.append(jnp.where(keep, qc, 0.0).astype(BF16))
    return _dot_nt(jnp.concatenate(qs, axis=0), kslab)


def _gqa_outputs(s, g, vslab, probs_fn):
    rows = s.shape[0] // 4
    kv_half = g % 2
    ps, denoms = [], []
    for i in range(4):
        p, den = probs_fn(i, s[i * rows:(i + 1) * rows])
        ps.append(p.astype(BF16))
        denoms.append(den)
    pv = _dot(jnp.concatenate(ps, axis=0), vslab)
    outs = []
    for i in range(4):
        o = pv[i * rows:(i + 1) * rows] / denoms[i]
        if i % 2 != kv_half:
            o = pltpu.roll(o, ATT_HEAD_DIM, 1)
        outs.append(o)
    low = _half_mask((rows, LANES), 0)
    return jnp.where(low, outs[0], outs[1]), jnp.where(low, outs[2], outs[3])


def _rope_cols(x, cos, sin_signed):
    lane = lax.broadcasted_iota(jnp.int32, x.shape, 1)
    first = (lane % ATT_HEAD_DIM) < (ATT_HEAD_DIM // 2)
    partner = jnp.where(first, pltpu.roll(x, LANES - ATT_HEAD_DIM // 2, 1), pltpu.roll(x, ATT_HEAD_DIM // 2, 1))
    return x * cos + partner * sin_signed


def _attn_window_kernel(sink_ref, q_ref, k_ref, v_ref, cos_ref, sin_ref, o_ref, kp_ref, vp_ref):
    t = k_ref.shape[0]
    step = pl.program_id(1)

    @pl.when(step == 0)
    def _():
        zeros = jnp.zeros((WINDOW, ATT_KV_W), BF16)
        kp_ref[0:WINDOW, :] = zeros
        kp_ref[WINDOW + t:, :] = zeros
        vp_ref[0:WINDOW, :] = zeros
        vp_ref[WINDOW + t:, :] = zeros
        for j in range(ATT_KV_W // LANES):
            sl = slice(j * LANES, (j + 1) * LANES)
            kp_ref[WINDOW:WINDOW + t, sl] = _rope_cols(k_ref[:, sl].astype(F32), cos_ref[...], sin_ref[...]).astype(BF16)
        vp_ref[WINDOW:WINDOW + t, :] = v_ref[...]

    for sub in range(WIN_STEP_BLOCKS):
        qrows = pl.ds(sub * WINDOW, WINDOW)
        _window_block(step * WIN_STEP_BLOCKS + sub, sink_ref, q_ref.at[qrows], cos_ref, sin_ref, o_ref.at[qrows],
                      kp_ref, vp_ref, t)


def _window_block(qb, sink_ref, q_ref, cos_ref, sin_ref, o_ref, kp_ref, vp_ref, t):
    blk = WINDOW
    band = blk + 2 * WINDOW
    r0 = pl.multiple_of(qb * blk, blk)
    cos = cos_ref[pl.ds(r0, blk), :]
    sin = sin_ref[pl.ds(r0, blk), :]
    scale = ATT_HEAD_DIM ** -0.5
    qcols = [_rope_cols(q_ref[:, j * LANES:(j + 1) * LANES].astype(F32), cos, sin) * scale
             for j in range(ATT_Q_W // LANES)]

    ii = lax.broadcasted_iota(jnp.int32, (blk, band), 0)
    jj = lax.broadcasted_iota(jnp.int32, (blk, band), 1)
    kpos = r0 - WINDOW + jj
    valid = (jj - ii >= 0) & (jj - ii <= 2 * WINDOW) & (kpos >= 0) & (kpos < t)

    slabs = [slice((g // 2) * LANES, (g // 2 + 1) * LANES) for g in range(ATT_KV_HEADS)]
    scores = [_gqa_scores(qcols, g, kp_ref[pl.ds(r0, band), slabs[g]]) for g in range(ATT_KV_HEADS)]
    for g in range(ATT_KV_HEADS):
        vslab = vp_ref[pl.ds(r0, band), slabs[g]]

        def probs(i, s, g=g):
            sk = sink_ref[4 * g + i]
            s = jnp.where(valid, s, NEG_BIG)
            m = jnp.maximum(jnp.max(s, axis=-1, keepdims=True), sk)
            p = jnp.exp(s - m)
            return p, jnp.sum(p, axis=-1, keepdims=True) + jnp.exp(sk - m)

        o0, o1 = _gqa_outputs(scores[g], g, vslab, probs)
        o_ref[:, (2 * g) * LANES:(2 * g + 1) * LANES] = o0.astype(o_ref.dtype)
        o_ref[:, (2 * g + 1) * LANES:(2 * g + 2) * LANES] = o1.astype(o_ref.dtype)


def attn_window(proj, sink, cos_t, sin_t, bsz, t):
    blk = WINDOW * WIN_STEP_BLOCKS
    nb = t // blk
    grid_spec = pltpu.PrefetchScalarGridSpec(
        num_scalar_prefetch=1,
        grid=(bsz, nb),
        in_specs=[pl.BlockSpec((blk, ATT_Q_W), lambda b, i, s: (b * nb + i, 0)),
                  pl.BlockSpec((t, ATT_KV_W), lambda b, i, s: (b, ATT_Q_W // ATT_KV_W)),
                  pl.BlockSpec((t, ATT_KV_W), lambda b, i, s: (b, ATT_Q_W // ATT_KV_W + 1)),
                  pl.BlockSpec((t, LANES), lambda b, i, s: (0, 0)),
                  pl.BlockSpec((t, LANES), lambda b, i, s: (0, 0))],
        out_specs=pl.BlockSpec((blk, ATT_Q_W), lambda b, i, s: (b * nb + i, 0)),
        scratch_shapes=[pltpu.VMEM((t + 2 * WINDOW, ATT_KV_W), BF16),
                        pltpu.VMEM((t + 2 * WINDOW, ATT_KV_W), BF16)],
    )
    return pl.pallas_call(
        _attn_window_kernel,
        grid_spec=grid_spec,
        out_shape=jax.ShapeDtypeStruct((bsz * t, ATT_Q_W), BF16),
        compiler_params=_cparams("parallel", "arbitrary"),
        name="attn_window",
    )(sink, proj, proj, proj, cos_t, sin_t)


def _rope_tables(t):
    half = ATT_HEAD_DIM // 2
    inv = ROPE_THETA ** (-jnp.arange(half, dtype=F32) / half)
    ang = jnp.arange(t).astype(F32)[:, None] * inv[None, :]
    cos = jnp.cos(ang)
    sin = jnp.sin(ang)
    reps = LANES // ATT_HEAD_DIM
    cos_t = jnp.tile(jnp.concatenate([cos, cos], axis=1), (1, reps))
    sin_t = jnp.tile(jnp.concatenate([-sin, sin], axis=1), (1, reps))
    return cos_t, sin_t


def _nbr_key_col_start(piece):
    return int(np.clip(piece * NA_QCOLS - NA_COLS // 2, 0, GRID_W - NA_KCOLS))


def _attn_nbr_kernel(q_ref, k_ref, v_ref, *rest, rows, kr):
    bias_refs = rest[:NBR_STEP_ROWS]
    o_ref, kf_ref, vf_ref = rest[NBR_STEP_ROWS:]
    step = pl.program_id(1)

    @pl.when(step == 0)
    def _():
        kf_ref[...] = k_ref[...].astype(F32)
        vf_ref[...] = v_ref[...].astype(F32)

    for sub in range(NBR_STEP_ROWS):
        qrows = pl.ds(sub * GRID_W, GRID_W)
        _nbr_query_row(step * NBR_STEP_ROWS + sub, q_ref.at[qrows], bias_refs[sub], o_ref.at[qrows], kf_ref, vf_ref,
                       rows, kr)


def _nbr_query_row(r, q_ref, bias_ref, o_ref, kf_ref, vf_ref, rows, kr):
    rs = jnp.clip(r - kr // 2, 0, rows - kr)
    scale = ATT_HEAD_DIM ** -0.5
    qcols = [q_ref[:, j * LANES:(j + 1) * LANES].astype(F32) * scale for j in range(ATT_Q_W // LANES)]
    n_pieces = GRID_W // NA_QCOLS
    n_slabs = ATT_KV_W // LANES

    def gather(ref, piece, slab):
        cs = _nbr_key_col_start(piece)
        parts = [ref[pl.ds(pl.multiple_of((rs + i) * GRID_W + cs, 8), NA_KCOLS), slab * LANES:(slab + 1) * LANES]
                 for i in range(kr)]
        return jnp.concatenate(parts, axis=0).astype(BF16)

    scores = {}
    for piece in range(n_pieces):
        qpiece = [qc[piece * NA_QCOLS:(piece + 1) * NA_QCOLS] for qc in qcols]
        kslabs = [gather(kf_ref, piece, slab) for slab in range(n_slabs)]
        for g in range(ATT_KV_HEADS):
            scores[piece, g] = _gqa_scores(qpiece, g, kslabs[g // 2])

    outs = [[None] * n_pieces for _ in range(ATT_Q_W // LANES)]
    for piece in range(n_pieces):
        vslabs = [gather(vf_ref, piece, slab) for slab in range(n_slabs)]
        for g in range(ATT_KV_HEADS):

            def probs(i, s, g=g, piece=piece):
                s = s + bias_ref[0, 4 * g + i, piece]
                p = jnp.exp(s - jnp.max(s, axis=-1, keepdims=True))
                return p, jnp.sum(p, axis=-1, keepdims=True)

            outs[2 * g][piece], outs[2 * g + 1][piece] = _gqa_outputs(scores[piece, g], g, vslabs[g // 2], probs)
    for j, pieces in enumerate(outs):
        o_ref[:, j * LANES:(j + 1) * LANES] = jnp.concatenate(pieces, axis=0).astype(o_ref.dtype)


def _nbr_bias_tables(rpb, kr):
    qc = np.arange(GRID_W)
    kc = np.arange(GRID_W)
    col_start = np.clip(qc - NA_COLS // 2, 0, GRID_W - NA_COLS)
    in_win = (kc[None, :] >= col_start[:, None]) & (kc[None, :] < col_start[:, None] + NA_COLS)
    dc = np.clip(kc[None, :] - qc[:, None] + NA_COLS - 1, 0, 2 * NA_COLS - 2)
    pick = ((dc[:, :, None] == np.arange(2 * NA_COLS - 1)) & in_win[:, :, None]).astype(np.float32)
    by_off = jnp.stack([rpb.astype(F32)[:, NA_ROWS - 1 - off:NA_ROWS - 1 - off + kr] for off in range(kr)])
    tab = jnp.einsum('ohid,qkd->ohqik', by_off, pick, precision=lax.Precision.HIGHEST)
    tab = tab + jnp.where(in_win, 0.0, NEG_BIG).astype(F32)[None, None, :, None, :]
    pieces = []
    for piece in range(GRID_W // NA_QCOLS):
        cs = _nbr_key_col_start(piece)
        part = tab[:, :, piece * NA_QCOLS:(piece + 1) * NA_QCOLS, :, cs:cs + NA_KCOLS]
        pieces.append(part.reshape(kr, ATT_HEADS, NA_QCOLS, kr * NA_KCOLS))
    return jnp.stack(pieces, axis=2)


def attn_nbr(proj, rpb, bsz, t):
    rows = t // GRID_W
    kr = min(NA_ROWS, rows)
    bias = _nbr_bias_tables(rpb, kr)

    def bias_spec(sub):
        def index(b, s):
            r = s * NBR_STEP_ROWS + sub
            return (r - jnp.clip(r - kr // 2, 0, rows - kr), 0, 0, 0, 0)
        return pl.BlockSpec((1,) + bias.shape[1:], index)

    qb = NBR_STEP_ROWS * GRID_W
    steps = rows // NBR_STEP_ROWS
    return pl.pallas_call(
        functools.partial(_attn_nbr_kernel, rows=rows, kr=kr),
        grid=(bsz, steps),
        in_specs=[pl.BlockSpec((qb, ATT_Q_W), lambda b, s: (b * steps + s, 0)),
                  pl.BlockSpec((t, ATT_KV_W), lambda b, s: (b, ATT_Q_W // ATT_KV_W)),
                  pl.BlockSpec((t, ATT_KV_W), lambda b, s: (b, ATT_Q_W // ATT_KV_W + 1))]
                 + [bias_spec(sub) for sub in range(NBR_STEP_ROWS)],
        out_specs=pl.BlockSpec((qb, ATT_Q_W), lambda b, s: (b * steps + s, 0)),
        out_shape=jax.ShapeDtypeStruct((bsz * t, ATT_Q_W), BF16),
        scratch_shapes=[pltpu.VMEM((t, ATT_KV_W), F32), pltpu.VMEM((t, ATT_KV_W), F32)],
        compiler_params=_cparams("parallel", "arbitrary"),
        name="attn_nbr",
    )(proj, proj, proj, *([bias] * NBR_STEP_ROWS))


def _scan_rows(x, reverse):
    n = x.shape[0]
    row = lax.broadcasted_iota(jnp.int32, x.shape, 0)
    s = 1
    while s < n:
        if reverse:
            x = x + jnp.where(row < n - s, pltpu.roll(x, n - s, 0), 0.0)
        else:
            x = x + jnp.where(row >= s, pltpu.roll(x, s, 0), 0.0)
        s *= 2
    return x


def _cumsum_rows_mxu(x, reverse):
    n = x.shape[0]
    ti = lax.broadcasted_iota(jnp.int32, (n, n), 0)
    si = lax.broadcasted_iota(jnp.int32, (n, n), 1)
    ones = jnp.where((si >= ti) if reverse else (si <= ti), 1.0, 0.0).astype(BF16)
    total, rest = None, x
    for _ in range(3):
        term = rest.astype(BF16)
        part = _dot(ones, term)
        total = part if total is None else total + part
        rest = rest - term.astype(F32)
    return total


def _gla_gates(q_ref, z_ref, lb, reverse):
    hq = 0.5 * q_ref[...].astype(F32)
    q = hq * jnp.tanh(hq) + hq
    half_gap = 0.5 * (1.0 - lb)
    f = (lb + half_gap) + half_gap * jnp.tanh(0.5 * z_ref[...])
    return q, 1.0 - f, _cumsum_rows_mxu(jnp.log2(f), reverse)


def _gla_direction(gates, v_ref, o_ref, st_ref, d, reverse):
    q, k, b = gates
    c, width = q.shape
    nsub = c // GLA_SUB
    v = v_ref[...]

    def row(i):
        return b[i:i + 1, :]

    zero = jnp.zeros((1, width), F32)
    if reverse:
        near = [row((i + 1) * GLA_SUB) for i in range(nsub - 1)] + [zero]
        far = [row(i * GLA_SUB) for i in range(nsub)]
        b_end = far[0]
    else:
        near = [zero] + [row(i * GLA_SUB - 1) for i in range(1, nsub)]
        far = [row((i + 1) * GLA_SUB - 1) for i in range(nsub)]
        b_end = far[nsub - 1]

    def per_block(rows):
        return jnp.concatenate([jnp.broadcast_to(r, (GLA_SUB, width)) for r in rows], axis=0)

    q_near = q * jnp.exp2(b - per_block(near))
    k_far = k * jnp.exp2(per_block(far) - b)

    def blocks(x, scales, zero_blocks):
        pieces = []
        for i in range(nsub):
            if i in zero_blocks:
                pieces.append(jnp.zeros((GLA_SUB, width), BF16))
            else:
                pieces.append((x[i * GLA_SUB:(i + 1) * GLA_SUB] * scales[i]).astype(BF16))
        return jnp.concatenate(pieces, axis=0)

    lhs, rhs = [], []
    for j in range(nsub):
        seen_by = range(0, j + 1) if reverse else range(j, nsub)
        scales = {i: jnp.exp2(near[i] - far[j]) for i in seen_by}
        lhs.append(blocks(q_near, scales, [i for i in range(nsub) if i not in scales]))
        k_j = k_far[j * GLA_SUB:(j + 1) * GLA_SUB].astype(BF16)
        zeros = jnp.zeros((GLA_SUB, width), BF16)
        rhs.append(jnp.concatenate([k_j if i == j else zeros for i in range(nsub)], axis=0))

    q_in = blocks(q_near, [jnp.exp2(near[i]) for i in range(nsub)], [])
    k_end = blocks(k_far, [jnp.exp2(b_end - far[i]) for i in range(nsub)], [])
    dec = jnp.exp2(b_end)

    ti = lax.broadcasted_iota(jnp.int32, (c, c), 0)
    si = lax.broadcasted_iota(jnp.int32, (c, c), 1)
    tri = (si >= ti) if reverse else (si <= ti)

    scores, inter = [], []
    for hd in range(HG_HEADS):
        sl = slice(hd * HG_DIM, (hd + 1) * HG_DIM)
        scores.append(_dot_nt(jnp.concatenate([p[:, sl] for p in lhs], axis=1),
                              jnp.concatenate([p[:, sl] for p in rhs], axis=1)))
        st = st_ref[d, hd]
        inter.append(_dot_nt(q_in[:, sl], st.astype(BF16)))
        st_ref[d, hd] = st * dec[:, sl] + _dot_tn(v[:, sl], k_end[:, sl])

    def finish():
        outs = []
        for hd in range(HG_HEADS):
            sl = slice(hd * HG_DIM, (hd + 1) * HG_DIM)
            a = jnp.where(tri, scores[hd], 0.0).astype(BF16)
            outs.append(_dot(a, v[:, sl]) + inter[hd])
        o_ref[...] = jnp.concatenate(outs, axis=1).astype(o_ref.dtype)

    return finish


def _gla_kernel(qf_ref, vf_ref, zf_ref, qb_ref, vb_ref, zb_ref, lb_ref, of_ref, ob_ref, st_ref):
    @pl.when(pl.program_id(1) == 0)
    def _():
        st_ref[...] = jnp.zeros_like(st_ref)

    c = GLA_CHUNK
    n_sub = qf_ref.shape[0] // c
    rows = [pl.ds(i * c, c) for i in range(n_sub)]
    gates_f = [_gla_gates(qf_ref.at[r], zf_ref.at[r], lb_ref[0:1, :], False) for r in rows]
    gates_b = [_gla_gates(qb_ref.at[r], zb_ref.at[r], lb_ref[1:2, :], True) for r in rows]
    finishes = []
    for i in range(n_sub):
        f, b = rows[i], rows[n_sub - 1 - i]
        finishes.append(_gla_direction(gates_f[i], vf_ref.at[f], of_ref.at[f], st_ref, 0, False))
        finishes.append(_gla_direction(gates_b[n_sub - 1 - i], vb_ref.at[b], ob_ref.at[b], st_ref, 1, True))
    for finish in finishes:
        finish()


def gla_bidir(proj, z, lb, bsz, t):
    c = min(GLA_CHUNK * GLA_STEP_CHUNKS, t)
    nch = t // c
    fwd = lambda col: pl.BlockSpec((c, HG_W), lambda b, j: (b * nch + j, col))
    bwd = lambda col: pl.BlockSpec((c, HG_W), lambda b, j: (b * nch + nch - 1 - j, col))
    out = jax.ShapeDtypeStruct((bsz * t, HG_W), BF16)
    return pl.pallas_call(
        _gla_kernel,
        grid=(bsz, nch),
        in_specs=[fwd(0), fwd(1), fwd(0), bwd(0), bwd(1), bwd(1), pl.BlockSpec((2, HG_W), lambda b, j: (0, 0))],
        out_specs=[fwd(0), bwd(0)],
        out_shape=[out, out],
        scratch_shapes=[pltpu.VMEM((2, HG_HEADS, HG_DIM, HG_DIM), F32)],
        compiler_params=_cparams("parallel", "arbitrary"),
        name="gla_bidir",
    )(proj, proj, z, proj, proj, z, lb)


def _expand_pair(cols, la, lb, rows):
    low = _half_mask((rows, LANES), 0)
    return jnp.where(low, jnp.broadcast_to(cols[:, la:la + 1], (rows, LANES)),
                     jnp.broadcast_to(cols[:, lb:lb + 1], (rows, LANES)))


def _ssd_direction(act_ref, dt_ref, dtb_ref, a_ref, dsk_ref, y_ref, st_ref, d, reverse):
    c = act_ref.shape[0]
    half = SSM_HEADS // 2
    base = d * SSM_HEADS
    xdt = dt_ref[...] + dtb_ref[...]
    dt = jnp.maximum(xdt, 0.0) + jnp.log(1.0 + jnp.exp(-jnp.abs(xdt)))
    cs = _scan_rows(dt * a_ref[...], reverse)
    end = 0 if reverse else c - 1
    cs_end = cs[end:end + 1, :]
    dec = jnp.exp(cs_end)

    def pair_rows(tab):
        tab_t = tab.T
        return jnp.concatenate([tab_t[base:base + half, :], tab_t[base + half:base + 2 * half, :]], axis=1)

    cs_rows = pair_rows(cs)
    dt_rows = pair_rows(dt)
    wd_rows = pair_rows(dt * jnp.exp(cs_end - cs))

    ti = lax.broadcasted_iota(jnp.int32, (c, LANES), 0)
    si = lax.broadcasted_iota(jnp.int32, (c, LANES), 1) % c
    tri = (si >= ti) if reverse else (si <= ti)
    low = _half_mask((c, LANES), 0)
    b_off = SSM_INNER
    c_off = SSM_INNER + SSM_GROUPS * SSM_STATE

    cbs, offs, bts = [], [], []
    for g in range(SSM_GROUPS):
        bm = act_ref[:, b_off + g * SSM_STATE:b_off + (g + 1) * SSM_STATE]
        cm = act_ref[:, c_off + g * SSM_STATE:c_off + (g + 1) * SSM_STATE]
        bm2 = jnp.concatenate([bm, bm], axis=0)
        cbs.append(_dot_nt(cm, bm2))
        bts.append(bm2.astype(F32).T.astype(BF16))
        offs.append(_dot(cm, st_ref[d, g].astype(BF16)))

    def finish():
        for p in range(2 * SSM_GROUPS):
            g, pp = p // 2, p % 2
            sl = slice(p * LANES, (p + 1) * LANES)
            psl = slice(pp * LANES, (pp + 1) * LANES)
            xs = act_ref[:, sl]
            r = jnp.concatenate([jnp.where(low, xs, jnp.zeros_like(xs)),
                                 jnp.where(low, jnp.zeros_like(xs), xs)], axis=0)
            e = _expand_pair(cs, base + p, base + half + p, c)
            decay = jnp.exp(jnp.where(tri, e - cs_rows[p:p + 1, :], NEG_BIG))
            m = (cbs[g] * decay * dt_rows[p:p + 1, :]).astype(BF16)
            w = bts[g] * wd_rows[p:p + 1, :].astype(BF16)
            both = _dot(jnp.concatenate([m, w], axis=0), r)
            y = both[0:c] + offs[g][:, psl] * jnp.exp(e)
            if not reverse:
                y = y + dsk_ref[:, sl] * xs.astype(F32)
            y_ref[:, sl] = y.astype(y_ref.dtype)
            st_ref[d, g, :, psl] = (st_ref[d, g, :, psl] * _expand_pair(dec, base + p, base + half + p, 1)
                                    + both[c:])

    return finish


def _ssd_kernel(af_ref, dtf_ref, ab_ref, dtb_in_ref, dtbias_ref, a_ref, dsk_ref, yf_ref, yb_ref, st_ref):
    @pl.when(pl.program_id(1) == 0)
    def _():
        st_ref[...] = jnp.zeros_like(st_ref)

    c = SSD_CHUNK
    n_sub = af_ref.shape[0] // c
    rows = [pl.ds(i * c, c) for i in range(n_sub)]
    for i in range(n_sub):
        f, b = rows[i], rows[n_sub - 1 - i]
        finish_f = _ssd_direction(af_ref.at[f], dtf_ref.at[f], dtbias_ref, a_ref, dsk_ref, yf_ref.at[f],
                                  st_ref, 0, False)
        finish_b = _ssd_direction(ab_ref.at[b], dtb_in_ref.at[b], dtbias_ref, a_ref, dsk_ref, yb_ref.at[b],
                                  st_ref, 1, True)
        finish_f()
        finish_b()


def ssd_bidir(act, dtp, dt_bias, a_neg, d_skip, bsz, t):
    c = min(SSD_CHUNK * SSD_STEP_CHUNKS, t)
    nch = t // c
    w = SSM_CONV_DIM
    fwd = lambda width: pl.BlockSpec((c, width), lambda b, j: (b * nch + j, 0))
    bwd = lambda width: pl.BlockSpec((c, width), lambda b, j: (b * nch + nch - 1 - j, 0))
    out = jax.ShapeDtypeStruct((bsz * t, SSM_INNER), BF16)
    const = lambda shape: pl.BlockSpec(shape, lambda b, j: (0,) * len(shape))
    return pl.pallas_call(
        _ssd_kernel,
        grid=(bsz, nch),
        in_specs=[fwd(w), fwd(LANES), bwd(w), bwd(LANES), const((1, LANES)), const((1, LANES)),
                  const((1, SSM_INNER))],
        out_specs=[fwd(SSM_INNER), bwd(SSM_INNER)],
        out_shape=[out, out],
        scratch_shapes=[pltpu.VMEM((2, SSM_GROUPS, SSM_STATE, 4 * SSM_HEAD_DIM), F32)],
        compiler_params=_cparams("parallel", "arbitrary"),
        name="ssd_bidir",
    )(act, dtp, act, dtp, dt_bias, a_neg, d_skip)


def _hgrn_lower_bounds(logits):
    p = jax.nn.softmax(logits.astype(F32), axis=1)
    return jnp.cumsum(p, axis=1) - p[:, :1]


def _pad_lanes(v, width=LANES):
    return jnp.pad(v, (0, width - v.shape[0])).reshape(1, width)


def kernel(x, mem, norm_g, mem_norm_g, w_mem_kv, a_w_in, a_sink, a_w_out, b_w_in, b_lb_logits, b_norm_g, b_w_out,
           c_w_in, c_conv_w, c_conv_b, c_dt_bias, c_a_log, c_d, c_norm_g, c_w_out, d_w_in, d_rpb, d_w_out,
           ffn_w_in, ffn_conv_w, ffn_conv_b, ffn_w_out):
    bsz, t, _ = x.shape
    n = bsz * t
    tm = min(ROW_TILE, t)
    depth = norm_g.shape[0]
    h = x.reshape(n, D_MODEL)
    mem2 = mem.reshape(-1, D_MODEL)
    cos_t, sin_t = _rope_tables(t)
    lb = _hgrn_lower_bounds(b_lb_logits)
    bf = lambda w: w.astype(BF16)

    kv_w = bf(jnp.transpose(w_mem_kv, (1, 0, 2)).reshape(D_MODEL, depth * 2 * X_W))
    (mem_kv,) = rms_matmul(mem2, mem_norm_g, kv_w, [(BF16, [(0, depth * 2 * X_W)])], min(ROW_TILE, mem2.shape[0]))

    for layer in range(depth):
        kind, slot = layer % 4, layer // 4
        g0, g1, g2, g3 = (norm_g[layer, i] for i in range(4))

        if kind == 0 or kind == 3:
            w_in = a_w_in[slot] if kind == 0 else d_w_in[slot]
            w_out = a_w_out[slot] if kind == 0 else d_w_out[slot]
            (proj,) = rms_matmul(h, g0, bf(w_in), [(BF16, [(0, w_in.shape[1])])], tm)
            if kind == 0:
                o = attn_window(proj, a_sink[slot].astype(F32), cos_t, sin_t, bsz, t)
            else:
                o = attn_nbr(proj, d_rpb[slot], bsz, t)
            xo = cross_attn(proj, (ATT_Q_W + 2 * ATT_KV_W) // X_W, mem_kv, layer, bsz, t)
            h = out_proj("plain", [(o, ATT_Q_W, 0)], xo, bf(w_out), h, g1, tm)
        elif kind == 1:
            proj, z = rms_matmul(h, g0, bf(b_w_in[slot]),
                                 [(BF16, [(0, 2 * HG_W), (4 * HG_W, 5 * HG_W + X_W)]), (F32, [(2 * HG_W, 4 * HG_W)])], tm)
            lbs = jnp.stack([lb[0, layer], lb[1, layer]])
            of, ob = gla_bidir(proj, z, lbs, bsz, t)
            xo = cross_attn(proj, 3 * HG_W // X_W, mem_kv, layer, bsz, t)
            h = out_proj("hgrn", [(of, HG_W, 0), (ob, HG_W, 0), (proj, HG_W, 2), (b_norm_g[slot].reshape(1, HG_W), None, 0)],
                         xo, bf(b_w_out[slot]), h, g1, tm)
        else:
            w_in = c_w_in[slot]
            z_w = w_in[:, :SSM_INNER]
            xbc_w = w_in[:, SSM_INNER:SSM_INNER + SSM_CONV_DIM]
            dt_w = w_in[:, SSM_INNER + SSM_CONV_DIM:SSM_INNER + SSM_CONV_DIM + 2 * SSM_HEADS]
            xq_w = w_in[:, SSM_INNER + SSM_CONV_DIM + 2 * SSM_HEADS:]
            lane_src = np.array([dr * SSM_HEADS + 2 * p + par for dr in range(2) for par in range(2)
                                 for p in range(SSM_HEADS // 2)])
            dt_pad = jnp.pad(dt_w[:, lane_src], ((0, 0), (0, LANES - 2 * SSM_HEADS)))
            proj, dtp = rms_matmul_conv(h, g0, bf(jnp.concatenate([xbc_w, z_w, xq_w, dt_pad], axis=1)),
                                        c_conv_w[slot].astype(F32), c_conv_b[slot].astype(F32).reshape(1, -1), t, tm)
            a_neg = -jnp.exp(c_a_log[slot].astype(F32)).reshape(-1)[lane_src]
            d_skip = jnp.repeat(c_d[slot].astype(F32), SSM_HEAD_DIM).reshape(1, SSM_INNER)
            yf, yb = ssd_bidir(proj, dtp, _pad_lanes(c_dt_bias[slot].astype(F32).reshape(-1)[lane_src]), _pad_lanes(a_neg),
                               d_skip, bsz, t)
            xo = cross_attn(proj, (SSM_CONV_DIM + SSM_INNER) // X_W, mem_kv, layer, bsz, t)
            h = out_proj("ssd", [(yf, SSM_INNER, 0), (yb, SSM_INNER, 0), (proj, SSM_INNER, SSM_CONV_DIM // SSM_INNER),
                                 (c_norm_g[slot].reshape(1, SSM_INNER), None, 0)],
                         xo, bf(c_w_out[slot]), h, g1, tm)

        h = conv_ffn(h, g2, g3, bf(ffn_w_in[layer]), ffn_conv_w[layer].astype(F32),
                     ffn_conv_b[layer].astype(F32), bf(ffn_w_out[layer]), t, min(FFN_ROW_TILE, t))
    return h.reshape(bsz, t, D_MODEL)
```

```python
import functools
import math

import jax
import jax.numpy as jnp
import numpy as np
from jax import lax
from jax.experimental import pallas as pl
from jax.experimental.pallas import tpu as pltpu

F32 = jnp.float32
BF16 = jnp.bfloat16

D_MODEL = 1024
NORM_EPS = 1e-6
GRID_W = 64
ATT_HEADS = 16
ATT_KV_HEADS = 4
ATT_HEAD_DIM = 64
ATT_Q_W = 1024
ATT_KV_W = 256
ROPE_THETA = 10000.0
WINDOW = 128
NA_ROWS = 8
NA_COLS = 16
NA_QCOLS = 64
NA_KCOLS = 64
NBR_STEP_ROWS = 1
WIN_STEP_BLOCKS = 1
X_HEADS = 4
X_HEAD_DIM = 128
X_W = 512
HG_HEADS = 8
HG_DIM = 128
HG_W = 1024
SSM_INNER = 2048
SSM_HEADS = 32
SSM_HEAD_DIM = 64
SSM_GROUPS = 8
SSM_STATE = 128
SSM_CONV = 5
SSM_CONV_DIM = 4096
D_FF = 2816
DEPTH = 4

LANES = 128
BF16_SUBLANES = 16
VMEM_LIMIT_BYTES = 56 * 1024 * 1024

ROW_TILE = 512
FFN_ROW_TILE = 1024
FFN_SPLIT = 1
COL_CHUNK = 512
FFN_CHUNK = 256
HALO = BF16_SUBLANES
GLA_CHUNK = 64
GLA_STEP_CHUNKS = 2
GLA_SUB = 16
SSD_CHUNK = LANES // 2
SSD_STEP_CHUNKS = 2
XATT_TILE = 512
NEG_BIG = -1e30


def _cparams(*sem):
    return pltpu.CompilerParams(dimension_semantics=sem, vmem_limit_bytes=VMEM_LIMIT_BYTES)


def _resident(shape):
    nd = len(shape)
    return pl.BlockSpec(shape, lambda *_: (0,) * nd, pipeline_mode=pl.Buffered(1))


def _rms_scale(x):
    return lax.rsqrt(jnp.mean(x * x, axis=-1, keepdims=True) + NORM_EPS)


def _sigmoid(x):
    return 0.5 * jnp.tanh(0.5 * x) + 0.5


def _silu(x):
    return x * _sigmoid(x)


def _dot(a, b):
    return jnp.dot(a, b, preferred_element_type=F32)


def _dot_nt(a, b):
    return lax.dot_general(a, b, (((1,), (1,)), ((), ())), preferred_element_type=F32)


def _dot_tn(a, b):
    return lax.dot_general(a, b, (((0,), (0,)), ((), ())), preferred_element_type=F32)


def _rms_matmul_kernel(x_ref, g_ref, w_ref, *o_refs, plan):
    x = x_ref[...]
    xn = (x * _rms_scale(x) * g_ref[...]).astype(BF16)
    for o_ref, ranges in zip(o_refs, plan):
        o0 = 0
        for start, stop in ranges:
            for c0 in range(start, stop, COL_CHUNK):
                c1 = min(c0 + COL_CHUNK, stop)
                o_ref[:, o0:o0 + c1 - c0] = _dot(xn, w_ref[:, c0:c1]).astype(o_ref.dtype)
                o0 += c1 - c0


def rms_matmul(x, g, w, outs, tm):
    m, k = x.shape
    widths = [sum(b - a for a, b in ranges) for _, ranges in outs]
    res = pl.pallas_call(
        functools.partial(_rms_matmul_kernel, plan=[ranges for _, ranges in outs]),
        grid=(m // tm,),
        in_specs=[pl.BlockSpec((tm, k), lambda i: (i, 0)),
                  _resident((1, k)),
                  _resident(w.shape)],
        out_specs=[pl.BlockSpec((tm, wd), lambda i: (i, 0)) for wd in widths],
        out_shape=[jax.ShapeDtypeStruct((m, wd), dt) for (dt, _), wd in zip(outs, widths)],
        compiler_params=_cparams("parallel"),
        name="rms_matmul",
    )(x, g.reshape(1, k), w)
    return res


def _rms_matmul_conv_kernel(x_ref, xp_ref, xn_ref, g_ref, w_ref, cw_ref, cb_ref, o_ref, tail_ref,
                            vn_ref, v_ref, vs_ref, pre_ref, res_ref, *, tiles_per_seq):
    tm = x_ref.shape[0]
    n_ext = tm + 2 * HALO
    nv = n_ext // 8
    tpos = pl.program_id(0) % tiles_per_seq
    pad = SSM_CONV // 2

    def normed(x):
        return x * _rms_scale(x) * g_ref[...]

    main = normed(x_ref[...])
    prev = jnp.where(tpos == 0, 0.0, normed(xp_ref[...]))
    nxt = jnp.where(tpos == tiles_per_seq - 1, 0.0, normed(xn_ref[...]))
    v_ref[...] = main.astype(BF16)
    n_lt = x_ref.shape[1] // LANES
    for lt in range(n_lt):
        lsl = slice(lt * LANES, (lt + 1) * LANES)
        vn_ref[lt, 0:HALO, :] = prev[:, lsl]
        vn_ref[lt, HALO:HALO + tm, :] = main[:, lsl]
        vn_ref[lt, HALO + tm:, :] = nxt[:, lsl]
    for j in range(0, nv, 2):
        pair = [jnp.concatenate([vn_ref[lt, pl.ds(j + dj, 8, stride=nv), :] for lt in range(n_lt)], axis=1)
                for dj in range(2)]
        vs_ref[8 * j:8 * j + 16, :] = jnp.concatenate(pair, axis=0).astype(BF16)

    chunk = 2 * LANES
    for ci, c0 in enumerate(range(0, SSM_CONV_DIM, chunk)):
        sl = slice(c0, c0 + chunk)
        pre = pre_ref.at[ci % 2]
        res = res_ref.at[ci % 2]
        pre[8 * pad:8 * pad + n_ext, :] = _dot(vs_ref[...], w_ref[:, sl])
        for i in range(pad):
            pre[8 * i:8 * i + 8, :] = pltpu.roll(pre[8 * (nv + i):8 * (nv + i) + 8, :], 1, 0)
            pre[8 * (pad + nv + i):8 * (pad + nv + i) + 8, :] = pltpu.roll(pre[8 * (pad + i):8 * (pad + i) + 8, :], 7, 0)
        acc = cb_ref[:, sl] + cw_ref[0:1, sl] * pre[0:n_ext, :]
        for kk in range(1, SSM_CONV):
            acc = acc + cw_ref[kk:kk + 1, sl] * pre[8 * kk:8 * kk + n_ext, :]
        act = _silu(acc)
        for l in range(chunk // LANES):
            for j in range(nv):
                res[l, pl.ds(j, 8, stride=nv), :] = act[8 * j:8 * j + 8, l * LANES:(l + 1) * LANES]
            o_ref[:, c0 + l * LANES:c0 + (l + 1) * LANES] = res[l, HALO:HALO + tm, :].astype(o_ref.dtype)
    width = o_ref.shape[1]
    for c0 in range(SSM_CONV_DIM, width, COL_CHUNK):
        c1 = min(c0 + COL_CHUNK, width)
        o_ref[:, c0:c1] = _dot(v_ref[...], w_ref[:, c0:c1]).astype(o_ref.dtype)
    tail_ref[...] = _dot(v_ref[...], w_ref[:, width:])


def rms_matmul_conv(x, g, w, conv_w, conv_b, t, tm):
    m, k = x.shape
    width = w.shape[1] - LANES
    tiles_per_seq = t // tm
    hb = tm // HALO
    last_hb = m // HALO - 1
    return pl.pallas_call(
        functools.partial(_rms_matmul_conv_kernel, tiles_per_seq=tiles_per_seq),
        grid=(m // tm,),
        in_specs=[pl.BlockSpec((tm, k), lambda i: (i, 0)),
                  pl.BlockSpec((HALO, k), lambda i: (jnp.maximum(i * hb - 1, 0), 0)),
                  pl.BlockSpec((HALO, k), lambda i: (jnp.minimum((i + 1) * hb, last_hb), 0)),
                  _resident((1, k)), _resident(w.shape), _resident(conv_w.shape), _resident(conv_b.shape)],
        out_specs=[pl.BlockSpec((tm, width), lambda i: (i, 0)), pl.BlockSpec((tm, LANES), lambda i: (i, 0))],
        out_shape=[jax.ShapeDtypeStruct((m, width), BF16), jax.ShapeDtypeStruct((m, LANES), F32)],
        scratch_shapes=[pltpu.VMEM((k // LANES, tm + 2 * HALO, LANES), F32),
                        pltpu.VMEM((tm, k), BF16),
                        pltpu.VMEM((tm + 2 * HALO, k), BF16),
                        pltpu.VMEM((2, tm + 2 * HALO + 16 * (SSM_CONV // 2), 2 * LANES), F32),
                        pltpu.VMEM((2, 2, tm + 2 * HALO, LANES), F32)],
        compiler_params=_cparams("parallel"),
        name="rms_matmul_conv",
    )(x, x, x, g.reshape(1, k), w, conv_w, conv_b)


def _xattn_kernel(q_ref, kv_ref, o_ref):
    scale = X_HEAD_DIM ** -0.5
    outs = []
    for h in range(X_HEADS):
        sl = slice(h * X_HEAD_DIM, (h + 1) * X_HEAD_DIM)
        q = q_ref[:, sl]
        k = kv_ref[:, sl]
        v = kv_ref[:, X_W + h * X_HEAD_DIM:X_W + (h + 1) * X_HEAD_DIM]
        s = _dot_nt(q, k) * scale
        p = jnp.exp(s - jnp.max(s, axis=-1, keepdims=True))
        l = jnp.sum(p, axis=-1, keepdims=True)
        outs.append(_dot(p.astype(BF16), v) / l)
    o_ref[...] = jnp.concatenate(outs, axis=1).astype(o_ref.dtype)


def cross_attn(proj, xq_block, mem_kv, kv_block, bsz, t):
    mem_len = mem_kv.shape[0] // bsz
    tq = min(XATT_TILE, t)
    nt = t // tq
    return pl.pallas_call(
        _xattn_kernel,
        grid=(bsz, nt),
        in_specs=[pl.BlockSpec((tq, X_W), lambda b, i: (b * nt + i, xq_block)),
                  pl.BlockSpec((mem_len, 2 * X_W), lambda b, i: (b, kv_block))],
        out_specs=pl.BlockSpec((tq, X_W), lambda b, i: (b * nt + i, 0)),
        out_shape=jax.ShapeDtypeStruct((bsz * t, X_W), BF16),
        compiler_params=_cparams("parallel", "parallel"),
        name="cross_attn",
    )(proj, mem_kv)


def _proj_norm_residual(o_bf16, xo_ref, w_ref, h_ref, g_ref, out_ref):
    k1 = o_bf16.shape[1]
    y = _dot(o_bf16, w_ref[0:k1, :]) + _dot(xo_ref[...], w_ref[k1:k1 + X_W, :])
    out_ref[...] = h_ref[...] + y * _rms_scale(y) * g_ref[...]


def _out_plain_kernel(o_ref, xo_ref, w_ref, h_ref, g_ref, out_ref):
    _proj_norm_residual(o_ref[...], xo_ref, w_ref, h_ref, g_ref, out_ref)


def _out_hgrn_kernel(of_ref, ob_ref, gate_ref, ng_ref, xo_ref, w_ref, h_ref, g_ref, out_ref):
    o = of_ref[...].astype(F32) + ob_ref[...].astype(F32)
    parts = []
    for hd in range(HG_HEADS):
        sl = slice(hd * HG_DIM, (hd + 1) * HG_DIM)
        oh = o[:, sl]
        on = oh * _rms_scale(oh) * ng_ref[:, sl]
        parts.append((on * _silu(gate_ref[:, sl].astype(F32))).astype(BF16))
    _proj_norm_residual(jnp.concatenate(parts, axis=1), xo_ref, w_ref, h_ref, g_ref, out_ref)


def _out_ssd_kernel(yf_ref, yb_ref, z_ref, ng_ref, xo_ref, w_ref, h_ref, g_ref, out_ref):
    gw = SSM_INNER // SSM_GROUPS
    parts = []
    for gi in range(SSM_GROUPS):
        sl = slice(gi * gw, (gi + 1) * gw)
        y = (yf_ref[:, sl].astype(F32) + yb_ref[:, sl].astype(F32)) * _silu(z_ref[:, sl].astype(F32))
        parts.append((y * _rms_scale(y) * ng_ref[:, sl]).astype(BF16))
    _proj_norm_residual(jnp.concatenate(parts, axis=1), xo_ref, w_ref, h_ref, g_ref, out_ref)


def _row_spec(tm, width, col_block=0):
    return pl.BlockSpec((tm, width), lambda i: (i, col_block))


def out_proj(kind, acts, xo, w, h, g, tm):
    n = h.shape[0]
    kernels = {"plain": _out_plain_kernel, "hgrn": _out_hgrn_kernel, "ssd": _out_ssd_kernel}
    in_specs, args = [], []
    for arr, width, cb in acts:
        if width is None:
            in_specs.append(_resident(arr.shape))
        else:
            in_specs.append(_row_spec(tm, width, cb))
        args.append(arr)
    in_specs += [_row_spec(tm, X_W), _resident(w.shape), _row_spec(tm, D_MODEL), _resident((1, D_MODEL))]
    args += [xo, w, h, g.reshape(1, D_MODEL)]
    return pl.pallas_call(
        kernels[kind],
        grid=(n // tm,),
        in_specs=in_specs,
        out_specs=_row_spec(tm, D_MODEL),
        out_shape=jax.ShapeDtypeStruct((n, D_MODEL), F32),
        compiler_params=_cparams("parallel"),
        name="out_proj_" + kind,
    )(*args)


def _gelu_tanh(x):
    c = math.sqrt(2.0 / math.pi)
    return x * (0.5 * (1.0 + jnp.tanh(c * (x + 0.044715 * (x * x * x)))))


def _ffn_kernel(h_ref, hp_ref, hn_ref, g2_ref, g3_ref, wi_ref, cw_ref, cb_ref, wo_ref, out_ref,
                v_ref, gate_ref, hid_ref, *, tiles_per_seq, n_chunks):
    tm = h_ref.shape[0]
    tpos = pl.program_id(0) % tiles_per_seq

    def normed(x):
        return (x * _rms_scale(x) * g2_ref[...]).astype(BF16)

    hm = tm // FFN_SPLIT
    v_ref[0:HALO, :] = jnp.where(tpos == 0, jnp.zeros((HALO, D_MODEL), BF16), normed(hp_ref[...]))
    for part in range(FFN_SPLIT):
        v_ref[HALO + part * hm:HALO + (part + 1) * hm, :] = normed(h_ref[part * hm:(part + 1) * hm, :])
    v_ref[HALO + tm:, :] = jnp.where(tpos == tiles_per_seq - 1, jnp.zeros((HALO, D_MODEL), BF16),
                                     normed(hn_ref[...]))

    for part in range(FFN_SPLIT):
        base = part * hm
        for c in range(n_chunks):
            sl = slice(c * FFN_CHUNK, (c + 1) * FFN_CHUNK)
            gate = gate_ref.at[(part * n_chunks + c) % 2]
            gate[...] = _dot(v_ref[base:base + hm + 2 * HALO, :], wi_ref[:, sl])
            up = _dot(v_ref[base + HALO:base + HALO + hm, :],
                      wi_ref[:, D_FF + c * FFN_CHUNK:D_FF + (c + 1) * FFN_CHUNK])
            conv = (cw_ref[0:1, sl] * gate[HALO - 1:HALO - 1 + hm, :]
                    + cw_ref[1:2, sl] * gate[HALO:HALO + hm, :]
                    + cw_ref[2:3, sl] * gate[HALO + 1:HALO + 1 + hm, :]
                    + cb_ref[:, sl])
            hid_ref[base:base + hm, sl] = (_gelu_tanh(conv) * up).astype(BF16)
        f = _dot(hid_ref[base:base + hm, :], wo_ref[...])
        out_ref[base:base + hm, :] = h_ref[base:base + hm, :] + f * _rms_scale(f) * g3_ref[...]


def conv_ffn(h, g2, g3, wi, cw, cb, wo, t, tm):
    n = h.shape[0]
    tiles_per_seq = t // tm
    hb = tm // HALO
    last_hb = n // HALO - 1
    kern = functools.partial(_ffn_kernel, tiles_per_seq=tiles_per_seq, n_chunks=D_FF // FFN_CHUNK)
    return pl.pallas_call(
        kern,
        grid=(n // tm,),
        in_specs=[pl.BlockSpec((tm, D_MODEL), lambda i: (i, 0)),
                  pl.BlockSpec((HALO, D_MODEL), lambda i: (jnp.maximum(i * hb - 1, 0), 0)),
                  pl.BlockSpec((HALO, D_MODEL), lambda i: (jnp.minimum((i + 1) * hb, last_hb), 0)),
                  _resident((1, D_MODEL)), _resident((1, D_MODEL)),
                  _resident(wi.shape), _resident(cw.shape), _resident((1, D_FF)), _resident(wo.shape)],
        out_specs=pl.BlockSpec((tm, D_MODEL), lambda i: (i, 0)),
        out_shape=jax.ShapeDtypeStruct((n, D_MODEL), F32),
        scratch_shapes=[pltpu.VMEM((tm + 2 * HALO, D_MODEL), BF16),
                        pltpu.VMEM((2, tm // FFN_SPLIT + 2 * HALO, FFN_CHUNK), F32),
                        pltpu.VMEM((tm, D_FF), BF16)],
        compiler_params=_cparams("parallel"),
        name="conv_ffn",
    )(h, h, h, g2.reshape(1, D_MODEL), g3.reshape(1, D_MODEL), wi, cw, cb.reshape(1, D_FF), wo)


def _half_mask(shape, half):
    lane = lax.broadcasted_iota(jnp.int32, shape, len(shape) - 1)
    return (lane >= ATT_HEAD_DIM) if half else (lane < ATT_HEAD_DIM)


def _gqa_scores(qcols, g, kslab):
    rows = qcols[0].shape[0]
    kv_half = g % 2
    keep = _half_mask((rows, LANES), kv_half)
    qs = []
    for i in range(4):
        qc = qcols[2 * g + i // 2]
        if i % 2 != kv_half:
            qc = pltpu.roll(qc, ATT_HEAD_DIM, 1)
        qs.append(jnp.where(keep, qc, 0.0).astype(BF16))
    return _dot_nt(jnp.concatenate(qs, axis=0), kslab)


def _gqa_outputs(s, g, vslab, probs_fn):
    rows = s.shape[0] // 4
    kv_half = g % 2
    ps, denoms = [], []
    for i in range(4):
        p, den = probs_fn(i, s[i * rows:(i + 1) * rows])
        ps.append(p.astype(BF16))
        denoms.append(den)
    pv = _dot(jnp.concatenate(ps, axis=0), vslab)
    outs = []
    for i in range(4):
        o = pv[i * rows:(i + 1) * rows] / denoms[i]
        if i % 2 != kv_half:
            o = pltpu.roll(o, ATT_HEAD_DIM, 1)
        outs.append(o)
    low = _half_mask((rows, LANES), 0)
    return jnp.where(low, outs[0], outs[1]), jnp.where(low, outs[2], outs[3])


def _rope_cols(x, cos, sin_signed):
    lane = lax.broadcasted_iota(jnp.int32, x.shape, 1)
    first = (lane % ATT_HEAD_DIM) < (ATT_HEAD_DIM // 2)
    partner = jnp.where(first, pltpu.roll(x, LANES - ATT_HEAD_DIM // 2, 1), pltpu.roll(x, ATT_HEAD_DIM // 2, 1))
    return x * cos + partner * sin_signed


def _attn_window_kernel(sink_ref, q_ref, k_ref, v_ref, cos_ref, sin_ref, o_ref, kp_ref, vp_ref):
    t = k_ref.shape[0]
    step = pl.program_id(1)

    @pl.when(step == 0)
    def _():
        zeros = jnp.zeros((WINDOW, ATT_KV_W), BF16)
        kp_ref[0:WINDOW, :] = zeros
        kp_ref[WINDOW + t:, :] = zeros
        vp_ref[0:WINDOW, :] = zeros
        vp_ref[WINDOW + t:, :] = zeros
        for j in range(ATT_KV_W // LANES):
            sl = slice(j * LANES, (j + 1) * LANES)
            kp_ref[WINDOW:WINDOW + t, sl] = _rope_cols(k_ref[:, sl].astype(F32), cos_ref[...], sin_ref[...]).astype(BF16)
        vp_ref[WINDOW:WINDOW + t, :] = v_ref[...]

    for sub in range(WIN_STEP_BLOCKS):
        qrows = pl.ds(sub * WINDOW, WINDOW)
        _window_block(step * WIN_STEP_BLOCKS + sub, sink_ref, q_ref.at[qrows], cos_ref, sin_ref, o_ref.at[qrows],
                      kp_ref, vp_ref, t)


def _window_block(qb, sink_ref, q_ref, cos_ref, sin_ref, o_ref, kp_ref, vp_ref, t):
    blk = WINDOW
    band = blk + 2 * WINDOW
    r0 = pl.multiple_of(qb * blk, blk)
    cos = cos_ref[pl.ds(r0, blk), :]
    sin = sin_ref[pl.ds(r0, blk), :]
    scale = ATT_HEAD_DIM ** -0.5
    qcols = [_rope_cols(q_ref[:, j * LANES:(j + 1) * LANES].astype(F32), cos, sin) * scale
             for j in range(ATT_Q_W // LANES)]

    ii = lax.broadcasted_iota(jnp.int32, (blk, band), 0)
    jj = lax.broadcasted_iota(jnp.int32, (blk, band), 1)
    kpos = r0 - WINDOW + jj
    valid = (jj - ii >= 0) & (jj - ii <= 2 * WINDOW) & (kpos >= 0) & (kpos < t)

    slabs = [slice((g // 2) * LANES, (g // 2 + 1) * LANES) for g in range(ATT_KV_HEADS)]
    scores = [_gqa_scores(qcols, g, kp_ref[pl.ds(r0, band), slabs[g]]) for g in range(ATT_KV_HEADS)]
    for g in range(ATT_KV_HEADS):
        vslab = vp_ref[pl.ds(r0, band), slabs[g]]

        def probs(i, s, g=g):
            sk = sink_ref[4 * g + i]
            s = jnp.where(valid, s, NEG_BIG)
            m = jnp.maximum(jnp.max(s, axis=-1, keepdims=True), sk)
            p = jnp.exp(s - m)
            return p, jnp.sum(p, axis=-1, keepdims=True) + jnp.exp(sk - m)

        o0, o1 = _gqa_outputs(scores[g], g, vslab, probs)
        o_ref[:, (2 * g) * LANES:(2 * g + 1) * LANES] = o0.astype(o_ref.dtype)
        o_ref[:, (2 * g + 1) * LANES:(2 * g + 2) * LANES] = o1.astype(o_ref.dtype)


def attn_window(proj, sink, cos_t, sin_t, bsz, t):
    blk = WINDOW * WIN_STEP_BLOCKS
    nb = t // blk
    grid_spec = pltpu.PrefetchScalarGridSpec(
        num_scalar_prefetch=1,
        grid=(bsz, nb),
        in_specs=[pl.BlockSpec((blk, ATT_Q_W), lambda b, i, s: (b * nb + i, 0)),
                  pl.BlockSpec((t, ATT_KV_W), lambda b, i, s: (b, ATT_Q_W // ATT_KV_W)),
                  pl.BlockSpec((t, ATT_KV_W), lambda b, i, s: (b, ATT_Q_W // ATT_KV_W + 1)),
                  pl.BlockSpec((t, LANES), lambda b, i, s: (0, 0)),
                  pl.BlockSpec((t, LANES), lambda b, i, s: (0, 0))],
        out_specs=pl.BlockSpec((blk, ATT_Q_W), lambda b, i, s: (b * nb + i, 0)),
        scratch_shapes=[pltpu.VMEM((t + 2 * WINDOW, ATT_KV_W), BF16),
                        pltpu.VMEM((t + 2 * WINDOW, ATT_KV_W), BF16)],
    )
    return pl.pallas_call(
        _attn_window_kernel,
        grid_spec=grid_spec,
        out_shape=jax.ShapeDtypeStruct((bsz * t, ATT_Q_W), BF16),
        compiler_params=_cparams("parallel", "arbitrary"),
        name="attn_window",
    )(sink, proj, proj, proj, cos_t, sin_t)


def _rope_tables(t):
    half = ATT_HEAD_DIM // 2
    inv = ROPE_THETA ** (-jnp.arange(half, dtype=F32) / half)
    ang = jnp.arange(t).astype(F32)[:, None] * inv[None, :]
    cos = jnp.cos(ang)
    sin = jnp.sin(ang)
    reps = LANES // ATT_HEAD_DIM
    cos_t = jnp.tile(jnp.concatenate([cos, cos], axis=1), (1, reps))
    sin_t = jnp.tile(jnp.concatenate([-sin, sin], axis=1), (1, reps))
    return cos_t, sin_t


def _nbr_key_col_start(piece):
    return int(np.clip(piece * NA_QCOLS - NA_COLS // 2, 0, GRID_W - NA_KCOLS))


def _attn_nbr_kernel(q_ref, k_ref, v_ref, *rest, rows, kr):
    bias_refs = rest[:NBR_STEP_ROWS]
    o_ref, kf_ref, vf_ref = rest[NBR_STEP_ROWS:]
    step = pl.program_id(1)

    @pl.when(step == 0)
    def _():
        kf_ref[...] = k_ref[...].astype(F32)
        vf_ref[...] = v_ref[...].astype(F32)

    for sub in range(NBR_STEP_ROWS):
        qrows = pl.ds(sub * GRID_W, GRID_W)
        _nbr_query_row(step * NBR_STEP_ROWS + sub, q_ref.at[qrows], bias_refs[sub], o_ref.at[qrows], kf_ref, vf_ref,
                       rows, kr)


def _nbr_query_row(r, q_ref, bias_ref, o_ref, kf_ref, vf_ref, rows, kr):
    rs = jnp.clip(r - kr // 2, 0, rows - kr)
    scale = ATT_HEAD_DIM ** -0.5
    qcols = [q_ref[:, j * LANES:(j + 1) * LANES].astype(F32) * scale for j in range(ATT_Q_W // LANES)]
    n_pieces = GRID_W // NA_QCOLS
    n_slabs = ATT_KV_W // LANES

    def gather(ref, piece, slab):
        cs = _nbr_key_col_start(piece)
        parts = [ref[pl.ds(pl.multiple_of((rs + i) * GRID_W + cs, 8), NA_KCOLS), slab * LANES:(slab + 1) * LANES]
                 for i in range(kr)]
        return jnp.concatenate(parts, axis=0).astype(BF16)

    scores = {}
    for piece in range(n_pieces):
        qpiece = [qc[piece * NA_QCOLS:(piece + 1) * NA_QCOLS] for qc in qcols]
        kslabs = [gather(kf_ref, piece, slab) for slab in range(n_slabs)]
        for g in range(ATT_KV_HEADS):
            scores[piece, g] = _gqa_scores(qpiece, g, kslabs[g // 2])

    outs = [[None] * n_pieces for _ in range(ATT_Q_W // LANES)]
    for piece in range(n_pieces):
        vslabs = [gather(vf_ref, piece, slab) for slab in range(n_slabs)]
        for g in range(ATT_KV_HEADS):

            def probs(i, s, g=g, piece=piece):
                s = s + bias_ref[0, 4 * g + i, piece]
                p = jnp.exp(s - jnp.max(s, axis=-1, keepdims=True))
                return p, jnp.sum(p, axis=-1, keepdims=True)

            outs[2 * g][piece], outs[2 * g + 1][piece] = _gqa_outputs(scores[piece, g], g, vslabs[g // 2], probs)
    for j, pieces in enumerate(outs):
        o_ref[:, j * LANES:(j + 1) * LANES] = jnp.concatenate(pieces, axis=0).astype(o_ref.dtype)


def _nbr_bias_tables(rpb, kr):
    qc = np.arange(GRID_W)
    kc = np.arange(GRID_W)
    col_start = np.clip(qc - NA_COLS // 2, 0, GRID_W - NA_COLS)
    in_win = (kc[None, :] >= col_start[:, None]) & (kc[None, :] < col_start[:, None] + NA_COLS)
    dc = np.clip(kc[None, :] - qc[:, None] + NA_COLS - 1, 0, 2 * NA_COLS - 2)
    pick = ((dc[:, :, None] == np.arange(2 * NA_COLS - 1)) & in_win[:, :, None]).astype(np.float32)
    by_off = jnp.stack([rpb.astype(F32)[:, NA_ROWS - 1 - off:NA_ROWS - 1 - off + kr] for off in range(kr)])
    tab = jnp.einsum('ohid,qkd->ohqik', by_off, pick, precision=lax.Precision.HIGHEST)
    tab = tab + jnp.where(in_win, 0.0, NEG_BIG).astype(F32)[None, None, :, None, :]
    pieces = []
    for piece in range(GRID_W // NA_QCOLS):
        cs = _nbr_key_col_start(piece)
        part = tab[:, :, piece * NA_QCOLS:(piece + 1) * NA_QCOLS, :, cs:cs + NA_KCOLS]
        pieces.append(part.reshape(kr, ATT_HEADS, NA_QCOLS, kr * NA_KCOLS))
    return jnp.stack(pieces, axis=2)


def attn_nbr(proj, rpb, bsz, t):
    rows = t // GRID_W
    kr = min(NA_ROWS, rows)
    bias = _nbr_bias_tables(rpb, kr)

    def bias_spec(sub):
        def index(b, s):
            r = s * NBR_STEP_ROWS + sub
            return (r - jnp.clip(r - kr // 2, 0, rows - kr), 0, 0, 0, 0)
        return pl.BlockSpec((1,) + bias.shape[1:], index)

    qb = NBR_STEP_ROWS * GRID_W
    steps = rows // NBR_STEP_ROWS
    return pl.pallas_call(
        functools.partial(_attn_nbr_kernel, rows=rows, kr=kr),
        grid=(bsz, steps),
        in_specs=[pl.BlockSpec((qb, ATT_Q_W), lambda b, s: (b * steps + s, 0)),
                  pl.BlockSpec((t, ATT_KV_W), lambda b, s: (b, ATT_Q_W // ATT_KV_W)),
                  pl.BlockSpec((t, ATT_KV_W), lambda b, s: (b, ATT_Q_W // ATT_KV_W + 1))]
                 + [bias_spec(sub) for sub in range(NBR_STEP_ROWS)],
        out_specs=pl.BlockSpec((qb, ATT_Q_W), lambda b, s: (b * steps + s, 0)),
        out_shape=jax.ShapeDtypeStruct((bsz * t, ATT_Q_W), BF16),
        scratch_shapes=[pltpu.VMEM((t, ATT_KV_W), F32), pltpu.VMEM((t, ATT_KV_W), F32)],
        compiler_params=_cparams("parallel", "arbitrary"),
        name="attn_nbr",
    )(proj, proj, proj, *([bias] * NBR_STEP_ROWS))


def _scan_rows(x, reverse):
    n = x.shape[0]
    row = lax.broadcasted_iota(jnp.int32, x.shape, 0)
    s = 1
    while s < n:
        if reverse:
            x = x + jnp.where(row < n - s, pltpu.roll(x, n - s, 0), 0.0)
        else:
            x = x + jnp.where(row >= s, pltpu.roll(x, s, 0), 0.0)
        s *= 2
    return x


def _cumsum_rows_mxu(x, reverse):
    n = x.shape[0]
    ti = lax.broadcasted_iota(jnp.int32, (n, n), 0)
    si = lax.broadcasted_iota(jnp.int32, (n, n), 1)
    ones = jnp.where((si >= ti) if reverse else (si <= ti), 1.0, 0.0).astype(BF16)
    total, rest = None, x
    for _ in range(3):
        term = rest.astype(BF16)
        part = _dot(ones, term)
        total = part if total is None else total + part
        rest = rest - term.astype(F32)
    return total


def _gla_gates(q_ref, z_ref, lb, reverse):
    hq = 0.5 * q_ref[...].astype(F32)
    q = hq * jnp.tanh(hq) + hq
    half_gap = 0.5 * (1.0 - lb)
    f = (lb + half_gap) + half_gap * jnp.tanh(0.5 * z_ref[...])
    return q, 1.0 - f, _cumsum_rows_mxu(jnp.log2(f), reverse)


def _gla_direction(gates, v_ref, o_ref, st_ref, d, reverse):
    q, k, b = gates
    c, width = q.shape
    nsub = c // GLA_SUB
    v = v_ref[...]

    def row(i):
        return b[i:i + 1, :]

    zero = jnp.zeros((1, width), F32)
    if reverse:
        near = [row((i + 1) * GLA_SUB) for i in range(nsub - 1)] + [zero]
        far = [row(i * GLA_SUB) for i in range(nsub)]
        b_end = far[0]
    else:
        near = [zero] + [row(i * GLA_SUB - 1) for i in range(1, nsub)]
        far = [row((i + 1) * GLA_SUB - 1) for i in range(nsub)]
        b_end = far[nsub - 1]

    def per_block(rows):
        return jnp.concatenate([jnp.broadcast_to(r, (GLA_SUB, width)) for r in rows], axis=0)

    q_near = q * jnp.exp2(b - per_block(near))
    k_far = k * jnp.exp2(per_block(far) - b)

    def blocks(x, scales, zero_blocks):
        pieces = []
        for i in range(nsub):
            if i in zero_blocks:
                pieces.append(jnp.zeros((GLA_SUB, width), BF16))
            else:
                pieces.append((x[i * GLA_SUB:(i + 1) * GLA_SUB] * scales[i]).astype(BF16))
        return jnp.concatenate(pieces, axis=0)

    lhs, rhs = [], []
    for j in range(nsub):
        seen_by = range(0, j + 1) if reverse else range(j, nsub)
        scales = {i: jnp.exp2(near[i] - far[j]) for i in seen_by}
        lhs.append(blocks(q_near, scales, [i for i in range(nsub) if i not in scales]))
        k_j = k_far[j * GLA_SUB:(j + 1) * GLA_SUB].astype(BF16)
        zeros = jnp.zeros((GLA_SUB, width), BF16)
        rhs.append(jnp.concatenate([k_j if i == j else zeros for i in range(nsub)], axis=0))

    q_in = blocks(q_near, [jnp.exp2(near[i]) for i in range(nsub)], [])
    k_end = blocks(k_far, [jnp.exp2(b_end - far[i]) for i in range(nsub)], [])
    dec = jnp.exp2(b_end)

    ti = lax.broadcasted_iota(jnp.int32, (c, c), 0)
    si = lax.broadcasted_iota(jnp.int32, (c, c), 1)
    tri = (si >= ti) if reverse else (si <= ti)

    scores, inter = [], []
    for hd in range(HG_HEADS):
        sl = slice(hd * HG_DIM, (hd + 1) * HG_DIM)
        scores.append(_dot_nt(jnp.concatenate([p[:, sl] for p in lhs], axis=1),
                              jnp.concatenate([p[:, sl] for p in rhs], axis=1)))
        st = st_ref[d, hd]
        inter.append(_dot_nt(q_in[:, sl], st.astype(BF16)))
        st_ref[d, hd] = st * dec[:, sl] + _dot_tn(v[:, sl], k_end[:, sl])

    def finish():
        outs = []
        for hd in range(HG_HEADS):
            sl = slice(hd * HG_DIM, (hd + 1) * HG_DIM)
            a = jnp.where(tri, scores[hd], 0.0).astype(BF16)
            outs.append(_dot(a, v[:, sl]) + inter[hd])
        o_ref[...] = jnp.concatenate(outs, axis=1).astype(o_ref.dtype)

    return finish


def _gla_kernel(qf_ref, vf_ref, zf_ref, qb_ref, vb_ref, zb_ref, lb_ref, of_ref, ob_ref, st_ref):
    @pl.when(pl.program_id(1) == 0)
    def _():
        st_ref[...] = jnp.zeros_like(st_ref)

    c = GLA_CHUNK
    n_sub = qf_ref.shape[0] // c
    rows = [pl.ds(i * c, c) for i in range(n_sub)]
    gates_f = [_gla_gates(qf_ref.at[r], zf_ref.at[r], lb_ref[0:1, :], False) for r in rows]
    gates_b = [_gla_gates(qb_ref.at[r], zb_ref.at[r], lb_ref[1:2, :], True) for r in rows]
    finishes = []
    for i in range(n_sub):
        f, b = rows[i], rows[n_sub - 1 - i]
        finishes.append(_gla_direction(gates_f[i], vf_ref.at[f], of_ref.at[f], st_ref, 0, False))
        finishes.append(_gla_direction(gates_b[n_sub - 1 - i], vb_ref.at[b], ob_ref.at[b], st_ref, 1, True))
    for finish in finishes:
        finish()


def gla_bidir(proj, z, lb, bsz, t):
    c = min(GLA_CHUNK * GLA_STEP_CHUNKS, t)
    nch = t // c
    fwd = lambda col: pl.BlockSpec((c, HG_W), lambda b, j: (b * nch + j, col))
    bwd = lambda col: pl.BlockSpec((c, HG_W), lambda b, j: (b * nch + nch - 1 - j, col))
    out = jax.ShapeDtypeStruct((bsz * t, HG_W), BF16)
    return pl.pallas_call(
        _gla_kernel,
        grid=(bsz, nch),
        in_specs=[fwd(0), fwd(1), fwd(0), bwd(0), bwd(1), bwd(1), pl.BlockSpec((2, HG_W), lambda b, j: (0, 0))],
        out_specs=[fwd(0), bwd(0)],
        out_shape=[out, out],
        scratch_shapes=[pltpu.VMEM((2, HG_HEADS, HG_DIM, HG_DIM), F32)],
        compiler_params=_cparams("parallel", "arbitrary"),
        name="gla_bidir",
    )(proj, proj, z, proj, proj, z, lb)


def _expand_pair(cols, la, lb, rows):
    low = _half_mask((rows, LANES), 0)
    return jnp.where(low, jnp.broadcast_to(cols[:, la:la + 1], (rows, LANES)),
                     jnp.broadcast_to(cols[:, lb:lb + 1], (rows, LANES)))


def _ssd_direction(act_ref, dt_ref, dtb_ref, a_ref, dsk_ref, y_ref, st_ref, d, reverse):
    c = act_ref.shape[0]
    half = SSM_HEADS // 2
    base = d * SSM_HEADS
    xdt = dt_ref[...] + dtb_ref[...]
    dt = jnp.maximum(xdt, 0.0) + jnp.log(1.0 + jnp.exp(-jnp.abs(xdt)))
    cs = _scan_rows(dt * a_ref[...], reverse)
    end = 0 if reverse else c - 1
    cs_end = cs[end:end + 1, :]
    dec = jnp.exp(cs_end)

    def pair_rows(tab):
        tab_t = tab.T
        return jnp.concatenate([tab_t[base:base + half, :], tab_t[base + half:base + 2 * half, :]], axis=1)

    cs_rows = pair_rows(cs)
    dt_rows = pair_rows(dt)
    wd_rows = pair_rows(dt * jnp.exp(cs_end - cs))

    ti = lax.broadcasted_iota(jnp.int32, (c, LANES), 0)
    si = lax.broadcasted_iota(jnp.int32, (c, LANES), 1) % c
    tri = (si >= ti) if reverse else (si <= ti)
    low = _half_mask((c, LANES), 0)
    b_off = SSM_INNER
    c_off = SSM_INNER + SSM_GROUPS * SSM_STATE

    cbs, offs, bts = [], [], []
    for g in range(SSM_GROUPS):
        bm = act_ref[:, b_off + g * SSM_STATE:b_off + (g + 1) * SSM_STATE]
        cm = act_ref[:, c_off + g * SSM_STATE:c_off + (g + 1) * SSM_STATE]
        bm2 = jnp.concatenate([bm, bm], axis=0)
        cbs.append(_dot_nt(cm, bm2))
        bts.append(bm2.astype(F32).T.astype(BF16))
        offs.append(_dot(cm, st_ref[d, g].astype(BF16)))

    def finish():
        for p in range(2 * SSM_GROUPS):
            g, pp = p // 2, p % 2
            sl = slice(p * LANES, (p + 1) * LANES)
            psl = slice(pp * LANES, (pp + 1) * LANES)
            xs = act_ref[:, sl]
            r = jnp.concatenate([jnp.where(low, xs, jnp.zeros_like(xs)),
                                 jnp.where(low, jnp.zeros_like(xs), xs)], axis=0)
            e = _expand_pair(cs, base + p, base + half + p, c)
            decay = jnp.exp(jnp.where(tri, e - cs_rows[p:p + 1, :], NEG_BIG))
            m = (cbs[g] * decay * dt_rows[p:p + 1, :]).astype(BF16)
            w = bts[g] * wd_rows[p:p + 1, :].astype(BF16)
            both = _dot(jnp.concatenate([m, w], axis=0), r)
            y = both[0:c] + offs[g][:, psl] * jnp.exp(e)
            if not reverse:
                y = y + dsk_ref[:, sl] * xs.astype(F32)
            y_ref[:, sl] = y.astype(y_ref.dtype)
            st_ref[d, g, :, psl] = (st_ref[d, g, :, psl] * _expand_pair(dec, base + p, base + half + p, 1)
                                    + both[c:])

    return finish


def _ssd_kernel(af_ref, dtf_ref, ab_ref, dtb_in_ref, dtbias_ref, a_ref, dsk_ref, yf_ref, yb_ref, st_ref):
    @pl.when(pl.program_id(1) == 0)
    def _():
        st_ref[...] = jnp.zeros_like(st_ref)

    c = SSD_CHUNK
    n_sub = af_ref.shape[0] // c
    rows = [pl.ds(i * c, c) for i in range(n_sub)]
    for i in range(n_sub):
        f, b = rows[i], rows[n_sub - 1 - i]
        finish_f = _ssd_direction(af_ref.at[f], dtf_ref.at[f], dtbias_ref, a_ref, dsk_ref, yf_ref.at[f],
                                  st_ref, 0, False)
        finish_b = _ssd_direction(ab_ref.at[b], dtb_in_ref.at[b], dtbias_ref, a_ref, dsk_ref, yb_ref.at[b],
                                  st_ref, 1, True)
        finish_f()
        finish_b()


def ssd_bidir(act, dtp, dt_bias, a_neg, d_skip, bsz, t):
    c = min(SSD_CHUNK * SSD_STEP_CHUNKS, t)
    nch = t // c
    w = SSM_CONV_DIM
    fwd = lambda width: pl.BlockSpec((c, width), lambda b, j: (b * nch + j, 0))
    bwd = lambda width: pl.BlockSpec((c, width), lambda b, j: (b * nch + nch - 1 - j, 0))
    out = jax.ShapeDtypeStruct((bsz * t, SSM_INNER), BF16)
    const = lambda shape: pl.BlockSpec(shape, lambda b, j: (0,) * len(shape))
    return pl.pallas_call(
        _ssd_kernel,
        grid=(bsz, nch),
        in_specs=[fwd(w), fwd(LANES), bwd(w), bwd(LANES), const((1, LANES)), const((1, LANES)),
                  const((1, SSM_INNER))],
        out_specs=[fwd(SSM_INNER), bwd(SSM_INNER)],
        out_shape=[out, out],
        scratch_shapes=[pltpu.VMEM((2, SSM_GROUPS, SSM_STATE, 4 * SSM_HEAD_DIM), F32)],
        compiler_params=_cparams("parallel", "arbitrary"),
        name="ssd_bidir",
    )(act, dtp, act, dtp, dt_bias, a_neg, d_skip)


def _hgrn_lower_bounds(logits):
    p = jax.nn.softmax(logits.astype(F32), axis=1)
    return jnp.cumsum(p, axis=1) - p[:, :1]


def _pad_lanes(v, width=LANES):
    return jnp.pad(v, (0, width - v.shape[0])).reshape(1, width)


def kernel(x, mem, norm_g, mem_norm_g, w_mem_kv, a_w_in, a_sink, a_w_out, b_w_in, b_lb_logits, b_norm_g, b_w_out,
           c_w_in, c_conv_w, c_conv_b, c_dt_bias, c_a_log, c_d, c_norm_g, c_w_out, d_w_in, d_rpb, d_w_out,
           ffn_w_in, ffn_conv_w, ffn_conv_b, ffn_w_out):
    bsz, t, _ = x.shape
    n = bsz * t
    tm = min(ROW_TILE, t)
    depth = norm_g.shape[0]
    h = x.reshape(n, D_MODEL)
    mem2 = mem.reshape(-1, D_MODEL)
    cos_t, sin_t = _rope_tables(t)
    lb = _hgrn_lower_bounds(b_lb_logits)
    bf = lambda w: w.astype(BF16)

    kv_w = bf(jnp.transpose(w_mem_kv, (1, 0, 2)).reshape(D_MODEL, depth * 2 * X_W))
    (mem_kv,) = rms_matmul(mem2, mem_norm_g, kv_w, [(BF16, [(0, depth * 2 * X_W)])], min(ROW_TILE, mem2.shape[0]))

    for layer in range(depth):
        kind, slot = layer % 4, layer // 4
        g0, g1, g2, g3 = (norm_g[layer, i] for i in range(4))

        if kind == 0 or kind == 3:
            w_in = a_w_in[slot] if kind == 0 else d_w_in[slot]
            w_out = a_w_out[slot] if kind == 0 else d_w_out[slot]
            (proj,) = rms_matmul(h, g0, bf(w_in), [(BF16, [(0, w_in.shape[1])])], tm)
            if kind == 0:
                o = attn_window(proj, a_sink[slot].astype(F32), cos_t, sin_t, bsz, t)
            else:
                o = attn_nbr(proj, d_rpb[slot], bsz, t)
            xo = cross_attn(proj, (ATT_Q_W + 2 * ATT_KV_W) // X_W, mem_kv, layer, bsz, t)
            h = out_proj("plain", [(o, ATT_Q_W, 0)], xo, bf(w_out), h, g1, tm)
        elif kind == 1:
            proj, z = rms_matmul(h, g0, bf(b_w_in[slot]),
                                 [(BF16, [(0, 2 * HG_W), (4 * HG_W, 5 * HG_W + X_W)]), (F32, [(2 * HG_W, 4 * HG_W)])], tm)
            lbs = jnp.stack([lb[0, layer], lb[1, layer]])
            of, ob = gla_bidir(proj, z, lbs, bsz, t)
            xo = cross_attn(proj, 3 * HG_W // X_W, mem_kv, layer, bsz, t)
            h = out_proj("hgrn", [(of, HG_W, 0), (ob, HG_W, 0), (proj, HG_W, 2), (b_norm_g[slot].reshape(1, HG_W), None, 0)],
                         xo, bf(b_w_out[slot]), h, g1, tm)
        else:
            w_in = c_w_in[slot]
            z_w = w_in[:, :SSM_INNER]
            xbc_w = w_in[:, SSM_INNER:SSM_INNER + SSM_CONV_DIM]
            dt_w = w_in[:, SSM_INNER + SSM_CONV_DIM:SSM_INNER + SSM_CONV_DIM + 2 * SSM_HEADS]
            xq_w = w_in[:, SSM_INNER + SSM_CONV_DIM + 2 * SSM_HEADS:]
            lane_src = np.array([dr * SSM_HEADS + 2 * p + par for dr in range(2) for par in range(2)
                                 for p in range(SSM_HEADS // 2)])
            dt_pad = jnp.pad(dt_w[:, lane_src], ((0, 0), (0, LANES - 2 * SSM_HEADS)))
            proj, dtp = rms_matmul_conv(h, g0, bf(jnp.concatenate([xbc_w, z_w, xq_w, dt_pad], axis=1)),
                                        c_conv_w[slot].astype(F32), c_conv_b[slot].astype(F32).reshape(1, -1), t, tm)
            a_neg = -jnp.exp(c_a_log[slot].astype(F32)).reshape(-1)[lane_src]
            d_skip = jnp.repeat(c_d[slot].astype(F32), SSM_HEAD_DIM).reshape(1, SSM_INNER)
            yf, yb = ssd_bidir(proj, dtp, _pad_lanes(c_dt_bias[slot].astype(F32).reshape(-1)[lane_src]), _pad_lanes(a_neg),
                               d_skip, bsz, t)
            xo = cross_attn(proj, (SSM_CONV_DIM + SSM_INNER) // X_W, mem_kv, layer, bsz, t)
            h = out_proj("ssd", [(yf, SSM_INNER, 0), (yb, SSM_INNER, 0), (proj, SSM_INNER, SSM_CONV_DIM // SSM_INNER),
                                 (c_norm_g[slot].reshape(1, SSM_INNER), None, 0)],
                         xo, bf(c_w_out[slot]), h, g1, tm)

        h = conv_ffn(h, g2, g3, bf(ffn_w_in[layer]), ffn_conv_w[layer].astype(F32),
                     ffn_conv_b[layer].astype(F32), bf(ffn_w_out[layer]), t, min(FFN_ROW_TILE, t))
    return h.reshape(bsz, t, D_MODEL)
```
